```python
import math
import functools
import jax
import jax.numpy as jnp
from jax import lax
import numpy as np

D_MODEL = 1024
BATCH = 4
SEQ = 8192
DEPTH = 1
DEC_BATCH = 128
DEC_SEQ = 8
PAST_LEN = 8192
PAGE_SIZE = 128

D_INNER = 2 * D_MODEL
SSM_HEAD_DIM = 64
SSM_HEADS = D_INNER // SSM_HEAD_DIM
SSM_GROUPS = 4
SSM_HEADS_PER_GROUP = SSM_HEADS // SSM_GROUPS
D_STATE = 128
CONV_WIDTH = 4
CONV_DIM = D_INNER + 2 * SSM_GROUPS * D_STATE
SSD_CHUNK = 256
HEAD_DIM = 64
MOBA_HEADS = D_MODEL // HEAD_DIM
MOBA_KV_HEADS = 4
Q_PER_KV = MOBA_HEADS // MOBA_KV_HEADS
MOBA_BLOCK = 256
MOBA_TOPK = 3
Q_BLOCK = 128
NUM_BUCKETS = 32
MAX_DISTANCE = 2048
MEM_LEN = 256
MEM_HEADS = 4
MEM_HEAD_DIM = D_MODEL // MEM_HEADS
PEER_HEADS = 8
PEER_KEYS = 128
PEER_EXPERTS = PEER_KEYS * PEER_KEYS
PEER_QDIM = 256
PEER_HALF = PEER_QDIM // 2
PEER_TOPK = 16
PEER_TOKEN_BLOCK = 256
N_BRANCHES = 3
EPS = 1e-6

IN_WIDTHS = (D_INNER, CONV_DIM, SSM_HEADS, MOBA_HEADS * HEAD_DIM, MOBA_KV_HEADS * HEAD_DIM,
             MOBA_KV_HEADS * HEAD_DIM, MEM_HEADS * MEM_HEAD_DIM, N_BRANCHES * D_MODEL)
IN_TOTAL = sum(IN_WIDTHS)
IN_SPLITS = tuple(sum(IN_WIDTHS[:i + 1]) for i in range(len(IN_WIDTHS) - 1))

kernel_name = 'hybrid_ssd_moba_peer_decoder_step'


def _rmsnorm(x, g):
    xf = x.astype(jnp.float32)
    y = xf * lax.rsqrt(jnp.mean(xf * xf, axis=-1, keepdims=True) + EPS)
    return (y * g.astype(jnp.float32)).astype(x.dtype)


def _rel_bucket(dist):
    n = jnp.maximum(dist, 0)
    max_exact = NUM_BUCKETS // 2
    nf = jnp.maximum(n, 1).astype(jnp.float32)
    large = max_exact + (jnp.log(nf / max_exact) / math.log(MAX_DISTANCE / max_exact)
                         * (NUM_BUCKETS - max_exact)).astype(jnp.int32)
    large = jnp.minimum(large, NUM_BUCKETS - 1)
    return jnp.where(n < max_exact, n, large)


def _causal_conv(xbc, buf, w, b):
    xp = jnp.concatenate([buf.astype(xbc.dtype), xbc], axis=1)
    y = lax.conv_general_dilated(xp, w[:, None, :].astype(xp.dtype), window_strides=(1,),
                                 padding='VALID', dimension_numbers=('NWC', 'WIO', 'NWC'),
                                 feature_group_count=CONV_DIM)
    return y + b.astype(y.dtype), xp[:, xp.shape[1] - (CONV_WIDTH - 1):]


def _ssd(x, dt, a, bm, cm, h0):
    bt, t = x.shape[:2]
    q = math.gcd(t, SSD_CHUNK)
    nc = t // q
    f32 = jnp.float32
    g, hg = SSM_GROUPS, SSM_HEADS_PER_GROUP
    xs = x.astype(f32).reshape(bt, nc, q, g, hg, SSM_HEAD_DIM)
    dts = dt.astype(f32).reshape(bt, nc, q, g, hg)
    bs = bm.astype(f32).reshape(bt, nc, q, g, D_STATE)
    cs = cm.astype(f32).reshape(bt, nc, q, g, D_STATE)
    cum = jnp.cumsum(dts * a.astype(f32).reshape(g, hg), axis=2)
    seg = cum[:, :, :, None] - cum[:, :, None, :]
    causal = jnp.tril(jnp.ones((q, q), dtype=bool))[:, :, None, None]
    decay = jnp.exp(jnp.where(causal, seg, -jnp.inf))
    cb = jnp.einsum('bcign,bcjgn->bcijg', cs, bs)
    y_diag = jnp.einsum('bcijg,bcijgh,bcjgh,bcjghp->bcighp', cb, decay, dts, xs)
    to_end = jnp.exp(cum[:, :, -1:] - cum)
    chunk_states = jnp.einsum('bcjgn,bcjgh,bcjghp->bcghpn', bs, to_end * dts, xs)
    chunk_decay = jnp.exp(cum[:, :, -1])

    def step(h, inp):
        st, dec = inp
        return h * dec[..., None, None] + st, h

    h_init = h0.astype(f32).reshape(bt, g, hg, SSM_HEAD_DIM, D_STATE)
    h_last, h_in = lax.scan(step, h_init, (jnp.moveaxis(chunk_states, 1, 0),
                                           jnp.moveaxis(chunk_decay, 1, 0)))
    h_in = jnp.moveaxis(h_in, 0, 1)
    y_off = jnp.einsum('bcign,bcghpn,bcigh->bcighp', cs, h_in, jnp.exp(cum))
    y = (y_diag + y_off).reshape(bt, t, SSM_HEADS, SSM_HEAD_DIM)
    return y, h_last.reshape(bt, SSM_HEADS, SSM_HEAD_DIM, D_STATE).astype(h0.dtype)


def _moba_group(q, k_t, v_t, k_mean, q_pos0, bias_hb):
    tq = q.shape[0]
    nb = k_t.shape[1]
    f32 = jnp.float32
    scale = HEAD_DIM ** -0.5
    qg = q.reshape(tq, MOBA_KV_HEADS, Q_PER_KV, HEAD_DIM)
    q_pos = q_pos0 + jnp.arange(tq)
    own = q_pos0 // MOBA_BLOCK
    gate = jnp.einsum('tkgd,knd->tkgn', qg, k_mean).astype(f32)
    gate = jnp.where(jnp.arange(nb) < own, gate, -jnp.inf)
    _, idx = lax.top_k(gate, MOBA_TOPK)
    sel_ok = jnp.arange(MOBA_TOPK) < own
    kv_i = jnp.arange(MOBA_KV_HEADS)[None, :, None, None]
    g_i = jnp.arange(Q_PER_KV)[None, None, :, None]
    k_sel = k_t[kv_i, idx]
    v_sel = v_t[kv_i, idx]
    s_sel = jnp.einsum('tkgd,tkgjsd->tkgjs', qg, k_sel).astype(f32) * scale
    k_pos = idx[..., None] * MOBA_BLOCK + jnp.arange(MOBA_BLOCK)
    b_sel = bias_hb[kv_i[..., None], g_i[..., None],
                    _rel_bucket(q_pos[:, None, None, None, None] - k_pos)]
    s_sel = jnp.where(sel_ok[:, None], s_sel + b_sel, -jnp.inf)
    k_own = lax.dynamic_index_in_dim(k_t, own, axis=1, keepdims=False)
    v_own = lax.dynamic_index_in_dim(v_t, own, axis=1, keepdims=False)
    s_own = jnp.einsum('tkgd,ksd->tkgs', qg, k_own).astype(f32) * scale
    d_own = q_pos[:, None] - (own * MOBA_BLOCK + jnp.arange(MOBA_BLOCK))[None, :]
    b_own = jnp.transpose(bias_hb[:, :, _rel_bucket(d_own)], (2, 0, 1, 3))
    s_own = jnp.where((d_own >= 0)[:, None, None, :], s_own + b_own, -jnp.inf)
    n_sel = MOBA_TOPK * MOBA_BLOCK
    logits = jnp.concatenate([s_sel.reshape(tq, MOBA_KV_HEADS, Q_PER_KV, n_sel), s_own], axis=-1)
    p = jax.nn.softmax(logits, axis=-1).astype(v_t.dtype)
    p_sel = p[..., :n_sel].reshape(tq, MOBA_KV_HEADS, Q_PER_KV, MOBA_TOPK, MOBA_BLOCK)
    p_own = p[..., n_sel:]
    out = (jnp.einsum('tkgjs,tkgjsd->tkgd', p_sel, v_sel)
           + jnp.einsum('tkgs,ksd->tkgd', p_own, v_own))
    return out.reshape(tq, MOBA_HEADS * HEAD_DIM).astype(q.dtype)


def _to_blocks(a, nb):
    a = jnp.pad(a, ((0, nb * MOBA_BLOCK - a.shape[0]), (0, 0), (0, 0)))
    return a.reshape(nb, MOBA_BLOCK, MOBA_KV_HEADS, HEAD_DIM).transpose(2, 0, 1, 3)


def _moba_prompt(q, k, v, bias_hb):
    b, t = q.shape[:2]
    nb = max(-(-t // MOBA_BLOCK), MOBA_TOPK)
    kb = jax.vmap(functools.partial(_to_blocks, nb=nb))(k)
    vb = jax.vmap(functools.partial(_to_blocks, nb=nb))(v)
    km = jnp.mean(kb.astype(jnp.float32), axis=3).astype(k.dtype)
    nq = t // Q_BLOCK
    b_idx = jnp.repeat(jnp.arange(b), nq)
    s_idx = jnp.tile(jnp.arange(nq) * Q_BLOCK, b)

    def body(args):
        bi, s0 = args
        qb = lax.dynamic_slice(q, (bi, s0, 0, 0), (1, Q_BLOCK, MOBA_HEADS, HEAD_DIM))[0]
        return _moba_group(qb, kb[bi], vb[bi], km[bi], s0, bias_hb)

    out = lax.map(body, (b_idx, s_idx))
    return out.reshape(b, t, MOBA_HEADS * HEAD_DIM)


def _moba_sample(q, k, v, cache_k, cache_v, page_table, bias_hb):
    s = q.shape[1]
    nb = max(-(-(PAST_LEN + s) // MOBA_BLOCK), MOBA_TOPK)

    def body(args):
        qi, ki, vi, pt = args
        k_full = jnp.concatenate([cache_k[pt].reshape(PAST_LEN, MOBA_KV_HEADS, HEAD_DIM).astype(ki.dtype), ki], axis=0)
        v_full = jnp.concatenate([cache_v[pt].reshape(PAST_LEN, MOBA_KV_HEADS, HEAD_DIM).astype(vi.dtype), vi], axis=0)
        kb = _to_blocks(k_full, nb)
        vb = _to_blocks(v_full, nb)
        km = jnp.mean(kb.astype(jnp.float32), axis=2).astype(kb.dtype)
        return _moba_group(qi, kb, vb, km, PAST_LEN, bias_hb)

    return lax.map(body, (q, k, v, page_table))


def _mem_kv(mem, g_norm, w_kv, g_k):
    b, m, _ = mem.shape
    kv = _rmsnorm(mem, g_norm) @ w_kv
    k, v = jnp.split(kv, 2, axis=-1)
    k = _rmsnorm(k.reshape(b, m, MEM_HEADS, MEM_HEAD_DIM), g_k)
    return k, v.reshape(b, m, MEM_HEADS, MEM_HEAD_DIM)


def _mem_attn(qm, mk, mv):
    bt, t = qm.shape[:2]
    s = jnp.einsum('bthd,bmhd->bhtm', qm, mk).astype(jnp.float32) * MEM_HEAD_DIM ** -0.5
    p = jax.nn.softmax(s, axis=-1).astype(mv.dtype)
    return jnp.einsum('bhtm,bmhd->bthd', p, mv).reshape(bt, t, MEM_HEADS * MEM_HEAD_DIM)


def _peer(h, w_pq, peer_keys, peer_u, peer_v):
    n = h.shape[0]
    blk = min(PEER_TOKEN_BLOCK, n)
    n_pad = -(-n // blk) * blk
    hp = jnp.pad(h, ((0, n_pad - n), (0, 0))).reshape(n_pad // blk, blk, D_MODEL)
    n_cand = PEER_TOPK * PEER_TOPK

    def body(hb):
        qv = (hb @ w_pq).reshape(blk, PEER_HEADS, 2, PEER_HALF)
        s = jnp.einsum('nhxd,hxkd->nhxk', qv, peer_keys).astype(jnp.float32)
        s1, i1 = lax.top_k(s[:, :, 0], PEER_TOPK)
        s2, i2 = lax.top_k(s[:, :, 1], PEER_TOPK)
        cand = (s1[..., :, None] + s2[..., None, :]).reshape(blk, PEER_HEADS, n_cand)
        cid = (i1[..., :, None] * PEER_KEYS + i2[..., None, :]).reshape(blk, PEER_HEADS, n_cand)
        top, pos = lax.top_k(cand, PEER_TOPK)
        eid = jnp.take_along_axis(cid, pos, axis=-1)
        g = jax.nn.softmax(top, axis=-1)
        act = jax.nn.gelu(jnp.einsum('nd,nhkd->nhk', hb, peer_u[eid]).astype(jnp.float32),
                          approximate=False)
        return jnp.einsum('nhk,nhkd->nd', (g * act).astype(hb.dtype), peer_v[eid])

    return lax.map(body, hp).reshape(n_pad, D_MODEL)[:n]


def _layer(x, w, ssm_h0, conv_buf, attn_fn, mem_k, mem_v):
    bt, t, _ = x.shape
    f32 = jnp.float32
    h = _rmsnorm(x, w['g_mix_norm'])
    proj = h @ w['w_in']
    z, xbc, dt_raw, q, k, v, qm, gate_pre = jnp.split(proj, IN_SPLITS, axis=-1)
    xbc, conv_new = _causal_conv(xbc, conv_buf, w['conv_w'], w['conv_b'])
    xbc = jax.nn.silu(xbc)
    xs, bm, cm = jnp.split(xbc, (D_INNER, D_INNER + SSM_GROUPS * D_STATE), axis=-1)
    xs = xs.reshape(bt, t, SSM_HEADS, SSM_HEAD_DIM)
    dt = jax.nn.softplus((dt_raw + w['dt_bias']).astype(f32))
    a = -jnp.exp(w['a_log'].astype(f32))
    y_ssd, ssm_new = _ssd(xs, dt, a, bm.reshape(bt, t, SSM_GROUPS, D_STATE),
                          cm.reshape(bt, t, SSM_GROUPS, D_STATE), ssm_h0)
    y_ssd = y_ssd + xs.astype(f32) * w['d_skip'].astype(f32)[:, None]
    y_ssd = y_ssd.reshape(bt, t, D_INNER).astype(x.dtype) * jax.nn.silu(z)
    y_ssd = _rmsnorm(y_ssd.reshape(bt, t, SSM_GROUPS, D_INNER // SSM_GROUPS),
                     w['g_ssm_norm'].reshape(SSM_GROUPS, D_INNER // SSM_GROUPS)).reshape(bt, t, D_INNER)
    q = _rmsnorm(q.reshape(bt, t, MOBA_HEADS, HEAD_DIM), w['g_q'])
    k = _rmsnorm(k.reshape(bt, t, MOBA_KV_HEADS, HEAD_DIM), w['g_k'])
    v = v.reshape(bt, t, MOBA_KV_HEADS, HEAD_DIM)
    y_moba = attn_fn(q, k, v)
    qm = _rmsnorm(qm.reshape(bt, t, MEM_HEADS, MEM_HEAD_DIM), w['g_mq'])
    y_mem = _mem_attn(qm, mem_k, mem_v)
    gates = jax.nn.sigmoid((gate_pre + w['b_gate']).astype(f32)).astype(x.dtype)
    g_ssd, g_moba, g_mem = jnp.split(gates, N_BRANCHES, axis=-1)
    merged = (g_ssd * (y_ssd @ w['w_ssm_out']) + g_moba * (y_moba @ w['w_moba_out'])
              + g_mem * (y_mem @ w['w_mem_out']))
    x1 = x + merged @ w['w_out']
    ff = _peer(_rmsnorm(x1, w['g_ffn_norm']).reshape(bt * t, D_MODEL),
               w['w_pq'], w['peer_keys'], w['peer_u'], w['peer_v'])
    return x1 + ff.reshape(bt, t, D_MODEL), ssm_new, conv_new, k, v


def setup_inputs(seed: int = 0) -> dict:
    key = jax.random.key(seed)
    ks = iter(jax.random.split(key, 48))
    f32 = jnp.float32

    def nrm(shape, scale):
        return jax.random.normal(next(ks), shape, f32) * scale

    def gain(shape):
        return 1.0 + 0.05 * jax.random.normal(next(ks), shape, f32)

    n_pages = PAST_LEN // PAGE_SIZE
    n_used = DEC_BATCH * n_pages
    n_pool = n_used + max(n_used // 4, 1)
    x_prompt = nrm((BATCH, SEQ, D_MODEL), 1.0)
    x_sample = nrm((DEC_BATCH, DEC_SEQ, D_MODEL), 1.0)
    cache_k = nrm((DEPTH, n_pool, PAGE_SIZE, MOBA_KV_HEADS, HEAD_DIM), 1.0)
    cache_v = nrm((DEPTH, n_pool, PAGE_SIZE, MOBA_KV_HEADS, HEAD_DIM), 1.0)
    state_ssm = nrm((DEPTH, DEC_BATCH, SSM_HEADS, SSM_HEAD_DIM, D_STATE), 0.1)
    state_conv = nrm((DEPTH, DEC_BATCH, CONV_WIDTH - 1, CONV_DIM), 1.0)
    cache_mem_k = nrm((DEPTH, DEC_BATCH, MEM_LEN, MEM_HEADS, MEM_HEAD_DIM), 1.0)
    cache_mem_v = nrm((DEPTH, DEC_BATCH, MEM_LEN, MEM_HEADS, MEM_HEAD_DIM), 1.0)
    page_table = jax.random.permutation(next(ks), n_pool)[:n_used].reshape(DEC_BATCH, n_pages).astype(jnp.int32)
    mem_prompt = nrm((BATCH, MEM_LEN, D_MODEL), 1.0)
    dt0 = jnp.exp(jax.random.uniform(next(ks), (DEPTH, SSM_HEADS), f32, math.log(1e-3), math.log(1e-1)))
    dt_bias = dt0 + jnp.log(-jnp.expm1(-dt0))
    a_log = jnp.log(jax.random.uniform(next(ks), (DEPTH, SSM_HEADS), f32, 1.0, 16.0))
    return {
        'x_prompt': x_prompt, 'x_sample': x_sample,
        'cache_k': cache_k, 'cache_v': cache_v,
        'state_ssm': state_ssm, 'state_conv': state_conv,
        'cache_mem_k': cache_mem_k, 'cache_mem_v': cache_mem_v,
        'page_table': page_table, 'mem_prompt': mem_prompt,
        'g_mix_norm': gain((DEPTH, D_MODEL)),
        'w_in': nrm((DEPTH, D_MODEL, IN_TOTAL), D_MODEL ** -0.5),
        'conv_w': nrm((DEPTH, CONV_WIDTH, CONV_DIM), CONV_WIDTH ** -0.5),
        'conv_b': nrm((DEPTH, CONV_DIM), 0.02),
        'dt_bias': dt_bias,
        'a_log': a_log,
        'd_skip': gain((DEPTH, SSM_HEADS)),
        'g_ssm_norm': gain((DEPTH, D_INNER)),
        'w_ssm_out': nrm((DEPTH, D_INNER, D_MODEL), D_INNER ** -0.5),
        'g_q': gain((DEPTH, HEAD_DIM)),
        'g_k': gain((DEPTH, HEAD_DIM)),
        'rel_bias': nrm((NUM_BUCKETS, MOBA_HEADS), 0.2),
        'w_moba_out': nrm((DEPTH, MOBA_HEADS * HEAD_DIM, D_MODEL), (MOBA_HEADS * HEAD_DIM) ** -0.5),
        'g_mem_norm': gain((DEPTH, D_MODEL)),
        'w_mem_kv': nrm((DEPTH, D_MODEL, 2 * MEM_HEADS * MEM_HEAD_DIM), D_MODEL ** -0.5),
        'g_mq': gain((DEPTH, MEM_HEAD_DIM)),
        'g_mk': gain((DEPTH, MEM_HEAD_DIM)),
        'w_mem_out': nrm((DEPTH, MEM_HEADS * MEM_HEAD_DIM, D_MODEL), (MEM_HEADS * MEM_HEAD_DIM) ** -0.5),
        'b_gate': nrm((DEPTH, N_BRANCHES * D_MODEL), 0.1),
        'w_out': nrm((DEPTH, D_MODEL, D_MODEL), D_MODEL ** -0.5),
        'g_ffn_norm': gain((DEPTH, D_MODEL)),
        'w_pq': nrm((DEPTH, D_MODEL, PEER_HEADS * PEER_QDIM), D_MODEL ** -0.5),
        'peer_keys': nrm((DEPTH, PEER_HEADS, 2, PEER_KEYS, PEER_HALF), PEER_HALF ** -0.5),
        'peer_u': nrm((DEPTH, PEER_EXPERTS, D_MODEL), D_MODEL ** -0.5),
        'peer_v': nrm((DEPTH, PEER_EXPERTS, D_MODEL), PEER_HEADS ** -0.5),
    }


def reference(x_prompt, x_sample, cache_k, cache_v, state_ssm, state_conv, cache_mem_k, cache_mem_v,
              page_table, mem_prompt, g_mix_norm, w_in, conv_w, conv_b, dt_bias, a_log, d_skip,
              g_ssm_norm, w_ssm_out, g_q, g_k, rel_bias, w_moba_out, g_mem_norm, w_mem_kv, g_mq, g_mk,
              w_mem_out, b_gate, w_out, g_ffn_norm, w_pq, peer_keys, peer_u, peer_v):
    bias_hb = rel_bias.T.reshape(MOBA_KV_HEADS, Q_PER_KV, NUM_BUCKETS)
    xp, xs = x_prompt, x_sample
    bp = x_prompt.shape[0]
    kp_l, vp_l, ks_l, vs_l, hp_l, hs_l, cp_l, cs_l, mkp_l, mvp_l = ([] for _ in range(10))
    for l in range(DEPTH):
        w = {'g_mix_norm': g_mix_norm[l], 'w_in': w_in[l], 'conv_w': conv_w[l], 'conv_b': conv_b[l],
             'dt_bias': dt_bias[l], 'a_log': a_log[l], 'd_skip': d_skip[l], 'g_ssm_norm': g_ssm_norm[l],
             'w_ssm_out': w_ssm_out[l], 'g_q': g_q[l], 'g_k': g_k[l], 'w_moba_out': w_moba_out[l],
             'g_mq': g_mq[l], 'w_mem_out': w_mem_out[l], 'b_gate': b_gate[l], 'w_out': w_out[l],
             'g_ffn_norm': g_ffn_norm[l], 'w_pq': w_pq[l], 'peer_keys': peer_keys[l],
             'peer_u': peer_u[l], 'peer_v': peer_v[l]}
        mk_p, mv_p = _mem_kv(mem_prompt, g_mem_norm[l], w_mem_kv[l], g_mk[l])
        h0 = jnp.zeros((bp, SSM_HEADS, SSM_HEAD_DIM, D_STATE), xp.dtype)
        c0 = jnp.zeros((bp, CONV_WIDTH - 1, CONV_DIM), xp.dtype)
        prompt_attn = functools.partial(_moba_prompt, bias_hb=bias_hb)
        xp, h_p, c_p, k_p, v_p = _layer(xp, w, h0, c0, prompt_attn, mk_p, mv_p)
        sample_attn = functools.partial(_moba_sample, cache_k=cache_k[l], cache_v=cache_v[l],
                                        page_table=page_table, bias_hb=bias_hb)
        xs, h_s, c_s, k_s, v_s = _layer(xs, w, state_ssm[l], state_conv[l], sample_attn,
                                        cache_mem_k[l], cache_mem_v[l])
        kp_l.append(k_p); vp_l.append(v_p); ks_l.append(k_s); vs_l.append(v_s)
        hp_l.append(h_p); hs_l.append(h_s); cp_l.append(c_p); cs_l.append(c_s)
        mkp_l.append(mk_p); mvp_l.append(mv_p)
    k_prompt, v_prompt = jnp.stack(kp_l), jnp.stack(vp_l)
    k_sample, v_sample = jnp.stack(ks_l), jnp.stack(vs_l)
    ssm_prompt, ssm_sample = jnp.stack(hp_l), jnp.stack(hs_l)
    conv_prompt, conv_sample = jnp.stack(cp_l), jnp.stack(cs_l)
    mem_k_prompt, mem_v_prompt = jnp.stack(mkp_l), jnp.stack(mvp_l)
    return (xp, xs, k_prompt, v_prompt, k_sample, v_sample, ssm_prompt, ssm_sample,
            conv_prompt, conv_sample, mem_k_prompt, mem_v_prompt)
```

```python
import functools
import math

import numpy as np
import jax
import jax.numpy as jnp
from jax import lax
from jax.experimental import pallas as pl
from jax.experimental.pallas import tpu as pltpu

F32 = jnp.float32
BF16 = jnp.bfloat16
I32 = jnp.int32
NEG_INF = float("-inf")

D_MODEL = 1024
D_INNER = 2048
SSM_HEAD_DIM = 64
SSM_HEADS = 32
SSM_GROUPS = 4
D_STATE = 128
CONV_WIDTH = 4
CONV_DIM = D_INNER + 2 * SSM_GROUPS * D_STATE
SSD_CHUNK = 256
HEAD_DIM = 64
MOBA_HEADS = 16
MOBA_KV_HEADS = 4
Q_PER_KV = 4
KV_DIM = MOBA_KV_HEADS * HEAD_DIM
MOBA_BLOCK = 256
MOBA_TOPK = 3
Q_BLOCK = 128
NUM_BUCKETS = 32
MAX_DISTANCE = 2048
MEM_HEADS = 4
MEM_HEAD_DIM = 256
PEER_HEADS = 8
PEER_KEYS = 128
PEER_HALF = 128
PEER_TOPK = 16
N_BRANCHES = 3
EPS = 1e-6

LANES = 128
SUBLANES = 8
VMEM_LIMIT_BYTES = 56 * 1024 * 1024


def _cparams(*sem):
    return pltpu.CompilerParams(dimension_semantics=sem, vmem_limit_bytes=VMEM_LIMIT_BYTES)


def _vmem_full():
    return pl.BlockSpec(memory_space=pltpu.VMEM)


def _dot(a, b):
    return jnp.dot(a, b, preferred_element_type=F32)


def _split_bf16(x):
    hi = x.astype(BF16)
    lo = (x - hi.astype(F32)).astype(BF16)
    return hi, lo


def _group_indicators(dim, gsize):
    e = np.zeros((dim, LANES), np.float32)
    e[np.arange(dim), np.arange(dim) // gsize] = 1.0
    et2 = np.concatenate([e.T, e.T], axis=0)
    return jnp.asarray(e, BF16), jnp.asarray(et2, BF16)


def _group_rr(x, e_ref, et2_ref, gsize):
    ssq = _dot((x * x).astype(BF16), e_ref[...])
    r = lax.rsqrt(ssq * (1.0 / gsize) + EPS)
    r_hi, r_lo = _split_bf16(r)
    return _dot(jnp.concatenate([r_hi, r_lo], axis=1), et2_ref[...])


def _inproj_kernel(x_ref, gmix_ref, wz_ref, wxbc_ref, wdt_ref, wq_ref, wk_ref, wv_ref, wqm_ref,
                   wg_ref, bgate_ref, gq_ref, gk_ref, gmq_ref, e64_ref, et64_ref, e64k_ref,
                   et64k_ref, e256_ref, et256_ref,
                   z_ref, xbc_ref, dt_ref, q_ref, k_ref, v_ref, qm_ref, gate_ref):
    x = x_ref[...]
    h = (x * lax.rsqrt(jnp.mean(x * x, axis=-1, keepdims=True) + EPS) * gmix_ref[...]).astype(BF16)
    z_ref[...] = _dot(h, wz_ref[...]).astype(BF16)
    xbc_ref[...] = _dot(h, wxbc_ref[...])
    dt_ref[...] = _dot(h, wdt_ref[...])
    q = _dot(h, wq_ref[...])
    q_ref[...] = (q * _group_rr(q, e64_ref, et64_ref, HEAD_DIM) * gq_ref[...]).astype(BF16)
    k = _dot(h, wk_ref[...])
    k_ref[...] = k * _group_rr(k, e64k_ref, et64k_ref, HEAD_DIM) * gk_ref[...]
    v_ref[...] = _dot(h, wv_ref[...])
    qm = _dot(h, wqm_ref[...])
    qm_ref[...] = (qm * _group_rr(qm, e256_ref, et256_ref, MEM_HEAD_DIM) * gmq_ref[...]).astype(BF16)
    gate_ref[...] = jax.nn.sigmoid(_dot(h, wg_ref[...]) + bgate_ref[...]).astype(BF16)


def _inproj(x, wts, tm):
    n = x.shape[0]
    assert n % tm == 0
    consts = wts["inproj"]
    row = lambda w: pl.BlockSpec((tm, w), lambda i: (i, 0))
    out_shape = (
        jax.ShapeDtypeStruct((n, D_INNER), BF16),
        jax.ShapeDtypeStruct((n, CONV_DIM), F32),
        jax.ShapeDtypeStruct((n, LANES), F32),
        jax.ShapeDtypeStruct((n, D_MODEL), BF16),
        jax.ShapeDtypeStruct((n, KV_DIM), F32),
        jax.ShapeDtypeStruct((n, KV_DIM), F32),
        jax.ShapeDtypeStruct((n, D_MODEL), BF16),
        jax.ShapeDtypeStruct((n, N_BRANCHES * D_MODEL), BF16),
    )
    return pl.pallas_call(
        _inproj_kernel,
        grid=(n // tm,),
        in_specs=[row(D_MODEL)] + [_vmem_full()] * len(consts),
        out_specs=tuple(row(s.shape[1]) for s in out_shape),
        out_shape=out_shape,
        compiler_params=_cparams("parallel"),
        name="inproj",
    )(x, *consts)


def _ssd_kernel(*refs, q_rows, n_valid, has_init):
    if has_init:
        (xbc_ref, z_ref, dt_ref, conv0_ref, ssm0_ref, cw_ref, cb_ref, dtb_ref, alog_ref, dskip_ref,
         gssm_ref, e2_ref, eg_ref, etg2_ref, y_ref, ssm_out_ref, conv_out_ref,
         xw_ref, st_ref, yb_ref) = refs
    else:
        (xbc_ref, z_ref, dt_ref, cw_ref, cb_ref, dtb_ref, alog_ref, dskip_ref,
         gssm_ref, e2_ref, eg_ref, etg2_ref, y_ref, ssm_out_ref, conv_out_ref,
         xw_ref, st_ref, yb_ref) = refs
    c = pl.program_id(1)
    last = pl.num_programs(1) - 1
    Q, NV = q_rows, n_valid
    G = SSM_GROUPS
    GW = D_INNER // G

    @pl.when(c == 0)
    def _init():
        if has_init:
            xw_ref[0:SUBLANES, :] = conv0_ref[0]
            st_ref[...] = ssm0_ref[0].T
        else:
            xw_ref[0:SUBLANES, :] = jnp.zeros((SUBLANES, CONV_DIM), F32)
            st_ref[...] = jnp.zeros_like(st_ref)

    xw_ref[SUBLANES:SUBLANES + NV, :] = xbc_ref[0]
    if NV < Q:
        xw_ref[SUBLANES + NV:SUBLANES + Q, :] = jnp.zeros((Q - NV, CONV_DIM), F32)

    acc = jnp.broadcast_to(cb_ref[...], (Q, CONV_DIM))
    for kk in range(CONV_WIDTH):
        off = SUBLANES - (CONV_WIDTH - 1) + kk
        acc = acc + cw_ref[kk:kk + 1, :] * xw_ref[off:off + Q, :]
    xc = jax.nn.silu(acc)
    xs = xc[:, :D_INNER]
    bm = xc[:, D_INNER:D_INNER + G * D_STATE]
    cm = xc[:, D_INNER + G * D_STATE:]

    dtv = dt_ref[0]
    if NV < Q:
        dtv = jnp.concatenate([dtv, jnp.zeros((Q - NV, LANES), F32)], axis=0)
    xdt_pre = dtv + dtb_ref[...]
    dt = jnp.maximum(xdt_pre, 0.0) + jnp.log1p(jnp.exp(-jnp.abs(xdt_pre)))
    row_i = lax.broadcasted_iota(I32, (Q, LANES), 0)
    if NV < Q:
        dt = jnp.where(row_i < NV, dt, 0.0)
    a = -jnp.exp(alog_ref[...])
    da = dt * a

    ri = lax.broadcasted_iota(I32, (Q, Q), 0)
    ci = lax.broadcasted_iota(I32, (Q, Q), 1)
    causal = ri >= ci
    lower = causal.astype(F32)
    upper = (ri <= ci).astype(F32)
    cum = jnp.dot(lower, da, preferred_element_type=F32, precision=lax.Precision.HIGHEST)
    cum_t = jnp.dot(da.T, upper, preferred_element_type=F32, precision=lax.Precision.HIGHEST)

    def expand(v):
        hi, lo = _split_bf16(v)
        return _dot(jnp.concatenate([hi, lo], axis=1), e2_ref[...])

    dt_x = expand(dt)
    ecum_x = expand(jnp.exp(cum))
    toend_x = expand(jnp.exp(cum[Q - 1:Q, :] - cum))
    xdt = xs * dt_x
    xdt_b = xdt.astype(BF16)
    xw_b = (xdt * toend_x).astype(BF16)
    dec_row = ecum_x[Q - 1:Q, :]
    lane_lo = lax.broadcasted_iota(I32, (Q, LANES), 1) < SSM_HEAD_DIM

    for g in range(G):
        cg = cm[:, g * D_STATE:(g + 1) * D_STATE].astype(BF16)
        bg = bm[:, g * D_STATE:(g + 1) * D_STATE]
        bg_b = bg.astype(BF16)
        cb = lax.dot_general(cg, bg_b, (((1,), (1,)), ((), ())), preferred_element_type=F32)
        st_g = st_ref[:, g * GW:(g + 1) * GW]
        y_off = _dot(cg, st_g.astype(BF16))
        for pair in range(GW // LANES):
            col0 = g * GW + pair * LANES
            x2 = xdt_b[:, col0:col0 + LANES]
            halves = []
            for e in range(2):
                hh = col0 // SSM_HEAD_DIM + e
                seg = cum[:, hh:hh + 1] - cum_t[hh:hh + 1, :]
                lmat = jnp.exp(jnp.where(causal, seg, NEG_INF))
                halves.append(_dot((cb * lmat).astype(BF16), x2))
            yb_ref[:, col0:col0 + LANES] = jnp.where(lane_lo, halves[0], halves[1])
        yb_ref[:, g * GW:(g + 1) * GW] = (yb_ref[:, g * GW:(g + 1) * GW]
                                          + y_off * ecum_x[:, g * GW:(g + 1) * GW])
        st_ref[:, g * GW:(g + 1) * GW] = (st_g * dec_row[:, g * GW:(g + 1) * GW]
                                          + _dot(bg.T.astype(BF16), xw_b[:, g * GW:(g + 1) * GW]))

    y = yb_ref[...] + xs * dskip_ref[...]
    zf = z_ref[0].astype(F32)
    if NV < Q:
        zf = jnp.concatenate([zf, jnp.zeros((Q - NV, D_INNER), F32)], axis=0)
    yz = y * jax.nn.silu(zf)
    yn = yz * _group_rr(yz, eg_ref, etg2_ref, GW) * gssm_ref[...]
    y_ref[0] = yn[0:NV, :].astype(BF16)

    xw_ref[0:SUBLANES, :] = xw_ref[NV:NV + SUBLANES, :]

    @pl.when(c == last)
    def _fin():
        ssm_out_ref[0] = st_ref[...].T
        conv_out_ref[0] = xw_ref[SUBLANES - (CONV_WIDTH - 1):SUBLANES, :]


def _ssd(xbc, z, dt, conv0, ssm0, wts, q_rows, n_valid):
    b, t, _ = xbc.shape
    assert t % n_valid == 0 and n_valid >= CONV_WIDTH - 1
    nc = t // n_valid
    has_init = conv0 is not None
    consts = wts["ssd"]
    tile = lambda w: pl.BlockSpec((1, n_valid, w), lambda i, c: (i, c, 0))
    in_specs = [tile(CONV_DIM), tile(D_INNER), tile(LANES)]
    args = [xbc, z, dt]
    if has_init:
        in_specs += [pl.BlockSpec((1, SUBLANES, CONV_DIM), lambda i, c: (i, 0, 0)),
                     pl.BlockSpec((1, D_INNER, D_STATE), lambda i, c: (i, 0, 0))]
        args += [conv0, ssm0]
    in_specs += [_vmem_full()] * len(consts)
    out_shape = (jax.ShapeDtypeStruct((b, t, D_INNER), BF16),
                 jax.ShapeDtypeStruct((b, D_INNER, D_STATE), F32),
                 jax.ShapeDtypeStruct((b, CONV_WIDTH - 1, CONV_DIM), F32))
    out_specs = (tile(D_INNER),
                 pl.BlockSpec((1, D_INNER, D_STATE), lambda i, c: (i, 0, 0)),
                 pl.BlockSpec((1, CONV_WIDTH - 1, CONV_DIM), lambda i, c: (i, 0, 0)))
    return pl.pallas_call(
        functools.partial(_ssd_kernel, q_rows=q_rows, n_valid=n_valid, has_init=has_init),
        grid=(b, nc),
        in_specs=in_specs,
        out_specs=out_specs,
        out_shape=out_shape,
        scratch_shapes=[pltpu.VMEM((SUBLANES + q_rows, CONV_DIM), F32),
                        pltpu.VMEM((D_STATE, D_INNER), F32),
                        pltpu.VMEM((q_rows, D_INNER), F32)],
        compiler_params=_cparams("parallel", "arbitrary"),
        name="ssd_scan",
    )(*args, *consts)


def _rel_bucket(dist):
    n = jnp.maximum(dist, 0)
    max_exact = NUM_BUCKETS // 2
    nf = jnp.maximum(n, 1).astype(F32)
    large = max_exact + (jnp.log(nf / max_exact) / math.log(MAX_DISTANCE / max_exact)
                         * (NUM_BUCKETS - max_exact)).astype(I32)
    large = jnp.minimum(large, NUM_BUCKETS - 1)
    return jnp.where(n < max_exact, n, large)


def _far_distance():
    d = np.arange(1, 4 * MAX_DISTANCE, dtype=np.float64)
    large = 16 + np.floor(np.log(d / 16) / math.log(MAX_DISTANCE / 16) * 16)
    below = np.nonzero(large < NUM_BUCKETS - 1)[0]
    return int(d[below[-1]]) + 1 + 1


def _bias_from_dist(dist, rbl_ref):
    bucket = _rel_bucket(dist)
    val = jnp.zeros(dist.shape, F32)
    for b in range(NUM_BUCKETS):
        val = jnp.where(bucket == b, rbl_ref[b:b + 1, :], val)
    return jnp.where(dist >= 0, val, NEG_INF)


def _bias_prompt_kernel(rbl_ref, o_ref):
    di = pl.program_id(0)
    w = Q_PER_KV * Q_BLOCK
    j = lax.broadcasted_iota(I32, (MOBA_BLOCK, w), 0)
    i = lax.broadcasted_iota(I32, (MOBA_BLOCK, w), 1) % Q_BLOCK
    o_ref[0, 0] = _bias_from_dist(di * Q_BLOCK + i - j, rbl_ref.at[0]).astype(BF16)


def _bias_prompt(rel_bias, n_tiles):
    w = Q_PER_KV * Q_BLOCK
    rbl = jnp.repeat(rel_bias.reshape(NUM_BUCKETS, MOBA_KV_HEADS, Q_PER_KV).transpose(1, 0, 2),
                     Q_BLOCK, axis=2)
    return pl.pallas_call(
        _bias_prompt_kernel,
        grid=(n_tiles, MOBA_KV_HEADS),
        in_specs=[pl.BlockSpec((1, NUM_BUCKETS, w), lambda d, h: (h, 0, 0))],
        out_specs=pl.BlockSpec((1, 1, MOBA_BLOCK, w), lambda d, h: (d, h, 0, 0)),
        out_shape=jax.ShapeDtypeStruct((n_tiles, MOBA_KV_HEADS, MOBA_BLOCK, w), BF16),
        compiler_params=_cparams("parallel", "parallel"),
        name="moba_bias_prompt",
    )(rbl)


def _bias_sample_kernel(rbl_ref, o_ref, *, past_len, first_block, n_near, dec_seq):
    r = pl.program_id(0)
    kpos0 = jnp.where(r < n_near, (first_block + r) * MOBA_BLOCK, past_len)
    j = lax.broadcasted_iota(I32, (MOBA_BLOCK, LANES), 0)
    t = lax.broadcasted_iota(I32, (MOBA_BLOCK, LANES), 1) % dec_seq
    o_ref[0] = _bias_from_dist(past_len + t - (kpos0 + j), rbl_ref)


def _bias_sample(rel_bias, past_len, first_block, n_near, dec_seq):
    rbl = jnp.repeat(rel_bias, dec_seq, axis=1)
    return pl.pallas_call(
        functools.partial(_bias_sample_kernel, past_len=past_len, first_block=first_block,
                          n_near=n_near, dec_seq=dec_seq),
        grid=(n_near + 1,),
        in_specs=[_vmem_full()],
        out_specs=pl.BlockSpec((1, MOBA_BLOCK, LANES), lambda r: (r, 0, 0)),
        out_shape=jax.ShapeDtypeStruct((n_near + 1, MOBA_BLOCK, LANES), F32),
        compiler_params=_cparams("parallel"),
        name="moba_bias_sample",
    )(rbl)


def _kmean_kernel(k_ref, o_ref, *, nb):
    for b in range(nb):
        blk = k_ref[0, b * MOBA_BLOCK:(b + 1) * MOBA_BLOCK, :]
        o_ref[0, b:b + 1, :] = jnp.sum(blk, axis=0, keepdims=True) * (1.0 / MOBA_BLOCK)


def _kmean(k):
    b, t, _ = k.shape
    nb = t // MOBA_BLOCK
    return pl.pallas_call(
        functools.partial(_kmean_kernel, nb=nb),
        grid=(b,),
        in_specs=[pl.BlockSpec((1, t, KV_DIM), lambda i: (i, 0, 0))],
        out_specs=pl.BlockSpec((1, nb, KV_DIM), lambda i: (i, 0, 0)),
        out_shape=jax.ShapeDtypeStruct((b, nb, KV_DIM), F32),
        compiler_params=_cparams("parallel"),
        name="moba_kmean",
    )(k)


def _select_blocks(gate, own, n_sel_rows):
    nb = gate.shape[0]
    blk = lax.broadcasted_iota(I32, gate.shape, 0)
    sel = jnp.zeros(gate.shape, F32)
    for t in range(MOBA_TOPK):
        m = jnp.max(gate, axis=0, keepdims=True)
        pos = jnp.min(jnp.where(gate == m, blk, nb), axis=0, keepdims=True)
        hit = blk == pos
        sel = jnp.where(hit, jnp.maximum(sel, jnp.where(t < n_sel_rows, 1.0, 0.0)), sel)
        gate = jnp.where(hit, NEG_INF, gate)
    return sel, blk


def _moba_prompt_kernel(qt_ref, k_ref, vt_ref, km_ref, bias_ref, farb_ref, o_ref,
                        qx_ref, add_ref, m_ref, l_ref, acc_ref, *, n_near):
    a = pl.program_id(1)
    own = a // 2
    par = a % 2
    nb = k_ref.shape[1]
    hw = Q_PER_KV * Q_BLOCK
    w = MOBA_KV_HEADS * hw

    row_h = lax.broadcasted_iota(I32, (KV_DIM, Q_BLOCK), 0) // HEAD_DIM
    for h in range(MOBA_KV_HEADS):
        for g in range(Q_PER_KV):
            piece = jnp.where(row_h == h, qt_ref[0, g], jnp.zeros((), BF16))
            col = (h * Q_PER_KV + g) * Q_BLOCK
            qx_ref[:, col:col + Q_BLOCK] = piece

    km_hi, km_lo = _split_bf16(km_ref[0])
    gate = _dot(km_hi, qx_ref[...]) + _dot(km_lo, qx_ref[...])
    blk0 = lax.broadcasted_iota(I32, (nb, w), 0)
    gate = jnp.where(blk0 < own, gate, NEG_INF)
    sel, blk = _select_blocks(gate, own, own)
    far = (own - blk) > n_near
    add = jnp.where(sel > 0.0, jnp.where(far, farb_ref[...], 0.0), NEG_INF)
    add_ref[...] = jnp.where(blk == own, 0.0, add)

    m_ref[...] = jnp.full(m_ref.shape, NEG_INF, F32)
    l_ref[...] = jnp.zeros(l_ref.shape, F32)
    acc_ref[...] = jnp.zeros(acc_ref.shape, F32)

    def step(b, di):
        kb = k_ref[0, b]
        for h in range(MOBA_KV_HEADS):
            s = _dot(kb, qx_ref[:, h * hw:(h + 1) * hw])
            s = s + add_ref[pl.ds(b, 1), h * hw:(h + 1) * hw]
            if di is not None:
                s = s + bias_ref[di, h].astype(F32)
            m_prev = m_ref[h]
            m_new = jnp.maximum(m_prev, jnp.max(s, axis=0, keepdims=True))
            p = jnp.exp(s - m_new)
            alpha = jnp.exp(m_prev - m_new)
            l_ref[h] = alpha * l_ref[h] + jnp.sum(p, axis=0, keepdims=True)
            vth = vt_ref[0, b, h * HEAD_DIM:(h + 1) * HEAD_DIM, :]
            acc_ref[h] = alpha * acc_ref[h] + _dot(vth, p.astype(BF16))
            m_ref[h] = m_new

    step(own, par)

    def near_body(kk, carry):
        step(own - kk, par + 2 * kk)
        return carry

    lax.fori_loop(1, jnp.minimum(n_near, own) + 1, near_body, 0)

    def far_body(b, carry):
        step(b, None)
        return carry

    lax.fori_loop(0, jnp.maximum(own - n_near, 0), far_body, 0)

    for h in range(MOBA_KV_HEADS):
        o_ref[0, h] = (acc_ref[h] / l_ref[h]).astype(BF16)


def _moba_prompt(q, k, v, bias_tiles, farb, n_near):
    b, t, _ = q.shape
    assert t % MOBA_BLOCK == 0
    nb, nq = t // MOBA_BLOCK, t // Q_BLOCK
    hw = Q_PER_KV * Q_BLOCK
    w = MOBA_KV_HEADS * hw
    km = _kmean(k)
    qt = q.reshape(b, t, MOBA_KV_HEADS, Q_PER_KV, HEAD_DIM).transpose(0, 3, 2, 4, 1).reshape(
        b, Q_PER_KV, KV_DIM, t)
    kb = k.astype(BF16).reshape(b, nb, MOBA_BLOCK, KV_DIM)
    vt = v.astype(BF16).reshape(b, nb, MOBA_BLOCK, KV_DIM).transpose(0, 1, 3, 2)
    out = pl.pallas_call(
        functools.partial(_moba_prompt_kernel, n_near=n_near),
        grid=(b, nq),
        in_specs=[pl.BlockSpec((1, Q_PER_KV, KV_DIM, Q_BLOCK), lambda i, a: (i, 0, 0, a)),
                  pl.BlockSpec((1, nb, MOBA_BLOCK, KV_DIM), lambda i, a: (i, 0, 0, 0)),
                  pl.BlockSpec((1, nb, KV_DIM, MOBA_BLOCK), lambda i, a: (i, 0, 0, 0)),
                  pl.BlockSpec((1, nb, KV_DIM), lambda i, a: (i, 0, 0)),
                  _vmem_full(), _vmem_full()],
        out_specs=pl.BlockSpec((1, MOBA_KV_HEADS, HEAD_DIM, hw), lambda i, a: (i, 0, 0, a)),
        out_shape=jax.ShapeDtypeStruct((b, MOBA_KV_HEADS, HEAD_DIM, nq * hw), BF16),
        scratch_shapes=[pltpu.VMEM((KV_DIM, w), BF16),
                        pltpu.VMEM((nb, w), F32),
                        pltpu.VMEM((MOBA_KV_HEADS, 1, hw), F32),
                        pltpu.VMEM((MOBA_KV_HEADS, 1, hw), F32),
                        pltpu.VMEM((MOBA_KV_HEADS, HEAD_DIM, hw), F32)],
        compiler_params=_cparams("parallel", "arbitrary"),
        name="moba_prompt",
    )(qt, kb, vt, km, bias_tiles, farb)
    out = out.reshape(b, MOBA_KV_HEADS, HEAD_DIM, nq, Q_PER_KV, Q_BLOCK).transpose(0, 3, 5, 1, 4, 2)
    return out.reshape(b, t, D_MODEL)


def _moba_sample_kernel(pt_ref, qx_ref, *rest, pages_per_step, n_far, dec_seq):
    del pt_ref
    gp = pages_per_step
    k_refs, v_refs = rest[:gp], rest[gp:2 * gp]
    (knew_ref, vnew_ref, biass_ref, farb_ref, o_ref,
     st_ref, vt_ref, ks_ref, add_ref) = rest[2 * gp:]
    j = pl.program_id(1)
    nbp = st_ref.shape[0]
    n_near = nbp - n_far
    page = MOBA_BLOCK // 2
    qx = qx_ref[0]

    for i in range(gp):
        kp = k_refs[i][0]
        blk = j * (gp // 2) + i // 2
        half = i % 2
        st_ref[blk, half * page:(half + 1) * page, :] = _dot(kp.astype(BF16), qx)
        vt_ref[blk, :, half * page:(half + 1) * page] = v_refs[i][0].T.astype(BF16)
        ksum = jnp.sum(kp, axis=0, keepdims=True)
        if half == 0:
            ks_ref[pl.ds(blk, 1), :] = ksum
        else:
            ks_ref[pl.ds(blk, 1), :] = ks_ref[pl.ds(blk, 1), :] + ksum

    @pl.when(j == pl.num_programs(1) - 1)
    def _fin():
        km_hi, km_lo = _split_bf16(ks_ref[...] * (1.0 / MOBA_BLOCK))
        gate = _dot(km_hi, qx) + _dot(km_lo, qx)
        sel, blk = _select_blocks(gate, nbp, nbp)
        add_ref[...] = jnp.where(sel > 0.0, jnp.where(blk < n_far, farb_ref[...], 0.0), NEG_INF)

        s_own = _dot(knew_ref[0].astype(BF16), qx) + biass_ref[n_near, 0:Q_BLOCK, :]
        m = jnp.max(s_own, axis=0, keepdims=True)

        def far_logits(b, m):
            lg = st_ref[b] + add_ref[pl.ds(b, 1), :]
            st_ref[b] = lg
            return jnp.maximum(m, jnp.max(lg, axis=0, keepdims=True))

        m = lax.fori_loop(0, n_far, far_logits, m)
        for r in range(n_near):
            b = n_far + r
            lg = st_ref[b] + add_ref[b:b + 1, :] + biass_ref[r]
            st_ref[b] = lg
            m = jnp.maximum(m, jnp.max(lg, axis=0, keepdims=True))

        p_own = jnp.exp(s_own - m)
        l0 = jnp.sum(p_own, axis=0, keepdims=True)
        acc0 = _dot(vnew_ref[0].T.astype(BF16), p_own.astype(BF16))

        def weighted(b, carry):
            l, acc = carry
            p = jnp.exp(st_ref[b] - m)
            return l + jnp.sum(p, axis=0, keepdims=True), acc + _dot(vt_ref[b], p.astype(BF16))

        l, acc = lax.fori_loop(0, nbp, weighted, (l0, acc0))
        y = acc / l
        lane_h = lax.broadcasted_iota(I32, (HEAD_DIM, LANES), 1) // (Q_PER_KV * dec_seq)
        out = jnp.zeros((HEAD_DIM, LANES), F32)
        for h in range(MOBA_KV_HEADS):
            out = jnp.where(lane_h == h, y[h * HEAD_DIM:(h + 1) * HEAD_DIM, :], out)
        o_ref[0] = out


def _moba_sample(q, k_new, v_new, cache_k, cache_v, page_table, bias_tiles, farb, n_far,
                 pages_per_step):
    s, dec_seq, _ = q.shape
    n_pool, page, _ = cache_k.shape
    n_pages = page_table.shape[1]
    assert page * 2 == MOBA_BLOCK and n_pages % pages_per_step == 0 and pages_per_step % 2 == 0
    assert MOBA_HEADS * dec_seq == LANES and dec_seq <= Q_BLOCK
    nbp = n_pages // 2
    gp = pages_per_step
    q5 = q.reshape(s, dec_seq, MOBA_KV_HEADS, Q_PER_KV, HEAD_DIM)
    qx = jnp.einsum("sthgd,ph->spdhgt", q5, jnp.eye(MOBA_KV_HEADS, dtype=q.dtype)).reshape(
        s, KV_DIM, LANES)
    pad = ((0, 0), (0, Q_BLOCK - dec_seq), (0, 0))
    k_pad, v_pad = jnp.pad(k_new, pad), jnp.pad(v_new, pad)

    def page_spec(i):
        return pl.BlockSpec((1, page, KV_DIM), lambda b, j, pt: (pt[b, j * gp + i], 0, 0))

    seq_spec = lambda r, w: pl.BlockSpec((1, r, w), lambda b, j, pt: (b, 0, 0))
    grid_spec = pltpu.PrefetchScalarGridSpec(
        num_scalar_prefetch=1,
        grid=(s, n_pages // gp),
        in_specs=([seq_spec(KV_DIM, LANES)] + [page_spec(i) for i in range(gp)] * 2
                  + [seq_spec(Q_BLOCK, KV_DIM), seq_spec(Q_BLOCK, KV_DIM), _vmem_full(), _vmem_full()]),
        out_specs=seq_spec(HEAD_DIM, LANES),
        scratch_shapes=[pltpu.VMEM((nbp, MOBA_BLOCK, LANES), F32),
                        pltpu.VMEM((nbp, KV_DIM, MOBA_BLOCK), BF16),
                        pltpu.VMEM((nbp, KV_DIM), F32),
                        pltpu.VMEM((nbp, LANES), F32)],
    )
    out = pl.pallas_call(
        functools.partial(_moba_sample_kernel, pages_per_step=gp, n_far=n_far, dec_seq=dec_seq),
        grid_spec=grid_spec,
        out_shape=jax.ShapeDtypeStruct((s, HEAD_DIM, LANES), F32),
        compiler_params=_cparams("parallel", "arbitrary"),
        name="moba_sample",
    )(page_table, qx, *([cache_k] * gp), *([cache_v] * gp), k_pad, v_pad, bias_tiles, farb)
    out = out.reshape(s, HEAD_DIM, MOBA_KV_HEADS, Q_PER_KV, dec_seq).transpose(0, 4, 2, 3, 1)
    return out.reshape(s, dec_seq, D_MODEL).astype(BF16)


def _memkv_kernel(mem_ref, gn_ref, w_ref, gk_ref, e_ref, et_ref, k_ref, v_ref):
    x = mem_ref[...]
    h = (x * lax.rsqrt(jnp.mean(x * x, axis=-1, keepdims=True) + EPS) * gn_ref[...]).astype(BF16)
    kv = _dot(h, w_ref[...])
    k = kv[:, :D_MODEL]
    k_ref[...] = k * _group_rr(k, e_ref, et_ref, MEM_HEAD_DIM) * gk_ref[...]
    v_ref[...] = kv[:, D_MODEL:]


def _memkv(mem, wts):
    n = mem.shape[0]
    tm = min(n, 256)
    assert n % tm == 0
    consts = wts["memkv"]
    row = pl.BlockSpec((tm, D_MODEL), lambda i: (i, 0))
    return pl.pallas_call(
        _memkv_kernel,
        grid=(n // tm,),
        in_specs=[row] + [_vmem_full()] * len(consts),
        out_specs=(row, row),
        out_shape=(jax.ShapeDtypeStruct((n, D_MODEL), F32),) * 2,
        compiler_params=_cparams("parallel"),
        name="mem_kv",
    )(mem, *consts)


def _memattn_kernel(q_ref, k_ref, v_ref, o_ref):
    q = q_ref[0]
    for h in range(MEM_HEADS):
        sl = slice(h * MEM_HEAD_DIM, (h + 1) * MEM_HEAD_DIM)
        kh = k_ref[0, :, sl].astype(BF16)
        vh = v_ref[0, :, sl].astype(BF16)
        s = lax.dot_general(q[:, sl], kh, (((1,), (1,)), ((), ())), preferred_element_type=F32)
        p = jnp.exp(s - jnp.max(s, axis=-1, keepdims=True))
        l = jnp.sum(p, axis=-1, keepdims=True)
        o_ref[0, :, sl] = (_dot(p.astype(BF16), vh) / l).astype(BF16)


def _memattn(qm, mk, mv, tq):
    b, t, _ = qm.shape
    m = mk.shape[1]
    assert t % tq == 0
    qspec = pl.BlockSpec((1, tq, D_MODEL), lambda i, a: (i, a, 0))
    mspec = pl.BlockSpec((1, m, D_MODEL), lambda i, a: (i, 0, 0))
    return pl.pallas_call(
        _memattn_kernel,
        grid=(b, t // tq),
        in_specs=[qspec, mspec, mspec],
        out_specs=qspec,
        out_shape=jax.ShapeDtypeStruct((b, t, D_MODEL), BF16),
        compiler_params=_cparams("parallel", "arbitrary"),
        name="mem_attn",
    )(qm, mk, mv)


def _merge_kernel(x_ref, ys_ref, ym_ref, yc_ref, gate_ref, ws_ref, wm_ref, wc_ref, wo_ref, gf_ref,
                  x1_ref, hn_ref):
    g = gate_ref[...].astype(F32)
    merged = (g[:, :D_MODEL] * _dot(ys_ref[...], ws_ref[...])
              + g[:, D_MODEL:2 * D_MODEL] * _dot(ym_ref[...], wm_ref[...])
              + g[:, 2 * D_MODEL:] * _dot(yc_ref[...], wc_ref[...]))
    x1 = x_ref[...] + _dot(merged.astype(BF16), wo_ref[...])
    x1_ref[...] = x1
    hn_ref[...] = x1 * lax.rsqrt(jnp.mean(x1 * x1, axis=-1, keepdims=True) + EPS) * gf_ref[...]


def _merge(x, y_ssd, y_moba, y_mem, gate, wts, tm):
    n = x.shape[0]
    assert n % tm == 0
    consts = wts["merge"]
    row = lambda w: pl.BlockSpec((tm, w), lambda i: (i, 0))
    return pl.pallas_call(
        _merge_kernel,
        grid=(n // tm,),
        in_specs=[row(D_MODEL), row(D_INNER), row(D_MODEL), row(D_MODEL), row(N_BRANCHES * D_MODEL)]
                 + [_vmem_full()] * len(consts),
        out_specs=(row(D_MODEL), row(D_MODEL)),
        out_shape=(jax.ShapeDtypeStruct((n, D_MODEL), F32),) * 2,
        compiler_params=_cparams("parallel"),
        name="merge_out",
    )(x, y_ssd, y_moba, y_mem, gate, *consts)


def _topk_rows(vals, payload, k, v_ref, p_ref):
    r = vals.shape[0]
    rows = lax.broadcasted_iota(I32, vals.shape, 0)
    for i in range(k):
        m = jnp.max(vals, axis=0, keepdims=True)
        pos = jnp.min(jnp.where(vals == m, rows, r), axis=0, keepdims=True)
        hit = rows == pos
        if payload is None:
            p_ref[i:i + 1, :] = pos
        else:
            p_ref[i:i + 1, :] = jnp.max(jnp.where(hit, payload, -1), axis=0, keepdims=True)
        v_ref[i:i + 1, :] = m
        vals = jnp.where(hit, NEG_INF, vals)


def _peer_topk_kernel(hn_ref, wpq_ref, keys_ref, eid_ref, g_ref,
                      s1v_ref, s1i_ref, s2v_ref, s2i_ref, cand_ref, cid_ref, tv_ref, ti_ref):
    qv = _dot(hn_ref[...].astype(BF16), wpq_ref[...])
    K = PEER_TOPK
    for h in range(PEER_HEADS):
        for x, (sv_ref, si_ref) in enumerate(((s1v_ref, s1i_ref), (s2v_ref, s2i_ref))):
            c0 = (h * 2 + x) * PEER_HALF
            q_hi, q_lo = _split_bf16(qv[:, c0:c0 + PEER_HALF])
            ql = jnp.concatenate([q_hi, q_lo], axis=1)
            st = lax.dot_general(keys_ref[h * 2 + x], ql, (((1,), (1,)), ((), ())),
                                 preferred_element_type=F32)
            _topk_rows(st, None, K, sv_ref, si_ref)
        s1, i1, s2, i2 = s1v_ref[...], s1i_ref[...], s2v_ref[...], s2i_ref[...]
        for a in range(K):
            cand_ref[a * K:(a + 1) * K, :] = s1[a:a + 1, :] + s2
            cid_ref[a * K:(a + 1) * K, :] = i1[a:a + 1, :] * PEER_KEYS + i2
        _topk_rows(cand_ref[...], cid_ref[...], K, tv_ref, ti_ref)
        top = tv_ref[...]
        e = jnp.exp(top - top[0:1, :])
        g_ref[h * K:(h + 1) * K, :] = e / jnp.sum(e, axis=0, keepdims=True)
        eid_ref[h * K:(h + 1) * K, :] = ti_ref[...]


def _peer_topk(hn, wts, tm):
    n = hn.shape[0]
    assert n % tm == 0
    consts = wts["peer_topk"]
    rows = PEER_HEADS * PEER_TOPK
    col = pl.BlockSpec((rows, tm), lambda i: (0, i))
    K = PEER_TOPK
    return pl.pallas_call(
        _peer_topk_kernel,
        grid=(n // tm,),
        in_specs=[pl.BlockSpec((tm, D_MODEL), lambda i: (i, 0))] + [_vmem_full()] * len(consts),
        out_specs=(col, col),
        out_shape=(jax.ShapeDtypeStruct((rows, n), I32), jax.ShapeDtypeStruct((rows, n), F32)),
        scratch_shapes=[pltpu.VMEM((K, tm), F32), pltpu.VMEM((K, tm), I32),
                        pltpu.VMEM((K, tm), F32), pltpu.VMEM((K, tm), I32),
                        pltpu.VMEM((K * K, tm), F32), pltpu.VMEM((K * K, tm), I32),
                        pltpu.VMEM((K, tm), F32), pltpu.VMEM((K, tm), I32)],
        compiler_params=_cparams("parallel"),
        name="peer_topk",
    )(hn, *consts)


TABLE_ROWS = 4
HI_MASK = -65536


def _pack_table(tbl):
    e = tbl.shape[0]
    bits = lax.bitcast_convert_type(tbl.astype(BF16), jnp.uint16).astype(jnp.uint32)
    half = D_MODEL // 2
    word = bits[:, :half] | (bits[:, half:] << 16)
    return lax.bitcast_convert_type(word, I32).reshape(e, TABLE_ROWS, LANES)


def _unpack_row(x):
    lo = pltpu.bitcast(x << 16, F32)
    hi = pltpu.bitcast(x & HI_MASK, F32)
    return lo, hi


def _peer_u_kernel(eid_ref, h_ref, g_ref, u_ref, coef_ref, p_ref):
    tt = h_ref.shape[0]
    n_sel = eid_ref.shape[1]
    lane = lax.broadcasted_iota(I32, (n_sel, tt), 1)

    def token(t, act_t):
        hh = h_ref[t]
        h_lo, h_hi = hh[0:TABLE_ROWS], hh[TABLE_ROWS:]
        for j in range(n_sel):
            lo, hi = _unpack_row(u_ref[eid_ref[t, j]])
            p_ref[j] = lo * h_lo + hi * h_hi
        col = jnp.sum(jnp.sum(p_ref[...], axis=1), axis=-1, keepdims=True)
        return jnp.where(lane == t, col, act_t)

    act = lax.fori_loop(0, tt, token, jnp.zeros((n_sel, tt), F32))
    gelu = 0.5 * act * (1.0 + lax.erf(act * (1.0 / math.sqrt(2.0))))
    coef_ref[...] = g_ref[...] * gelu


def _peer_v_kernel(eid_ref, coef_ref, x1_ref, v_ref, o_ref, crep_ref):
    tt = x1_ref.shape[0]
    n_sel = eid_ref.shape[1]
    lane = lax.broadcasted_iota(I32, (n_sel, tt), 1)

    def token(t, carry):
        c = jnp.sum(jnp.where(lane == t, coef_ref[...], 0.0), axis=-1, keepdims=True)
        crep_ref[...] = jnp.broadcast_to(c, crep_ref.shape)
        acc_lo = jnp.zeros((TABLE_ROWS, LANES), F32)
        acc_hi = jnp.zeros((TABLE_ROWS, LANES), F32)
        for j in range(n_sel):
            lo, hi = _unpack_row(v_ref[eid_ref[t, j]])
            cj = crep_ref[j:j + 1, :]
            acc_lo = acc_lo + cj * lo
            acc_hi = acc_hi + cj * hi
        o_ref[t] = x1_ref[t] + jnp.concatenate([acc_lo, acc_hi], axis=0)
        return carry

    lax.fori_loop(0, tt, token, 0)


def _peer(x1, hn, wts, tm_topk, tt):
    n = x1.shape[0]
    assert n % tt == 0
    n_sel = PEER_HEADS * PEER_TOPK
    eid_t, g_t = _peer_topk(hn, wts, tm_topk)
    eid = eid_t.T
    tok3 = pl.BlockSpec((tt, SUBLANES, LANES), lambda i: (i, 0, 0))
    col = pl.BlockSpec((n_sel, tt), lambda i: (0, i))
    ids = pl.BlockSpec((tt, n_sel), lambda i: (i, 0), memory_space=pltpu.SMEM)
    coef = pl.pallas_call(
        _peer_u_kernel,
        grid=(n // tt,),
        in_specs=[ids, tok3, col, _vmem_full()],
        out_specs=col,
        out_shape=jax.ShapeDtypeStruct((n_sel, n), F32),
        scratch_shapes=[pltpu.VMEM((n_sel, TABLE_ROWS, LANES), F32)],
        compiler_params=_cparams("parallel"),
        name="peer_u",
    )(eid, hn.reshape(n, SUBLANES, LANES), g_t, wts["peer_u"])
    out = pl.pallas_call(
        _peer_v_kernel,
        grid=(n // tt,),
        in_specs=[ids, col, tok3, _vmem_full()],
        out_specs=tok3,
        out_shape=jax.ShapeDtypeStruct((n, SUBLANES, LANES), F32),
        scratch_shapes=[pltpu.VMEM((n_sel, LANES), F32)],
        compiler_params=_cparams("parallel"),
        name="peer_v",
    )(eid, coef, x1.reshape(n, SUBLANES, LANES), wts["peer_v"])
    return out.reshape(n, D_MODEL)


IN_WIDTHS = (D_INNER, CONV_DIM, SSM_HEADS, D_MODEL, KV_DIM, KV_DIM, D_MODEL, N_BRANCHES * D_MODEL)


def _row(v):
    return v.reshape(1, -1).astype(F32)


def _prepare(p):
    offs = np.cumsum((0,) + IN_WIDTHS)
    seg = [p["w_in"][:, offs[i]:offs[i + 1]].astype(BF16) for i in range(len(IN_WIDTHS))]
    wz, wxbc, wdt, wq, wk, wv, wqm, wg = seg
    wdt = jnp.pad(wdt, ((0, 0), (0, LANES - SSM_HEADS)))
    e64, et64 = _group_indicators(D_MODEL, HEAD_DIM)
    e64k, et64k = _group_indicators(KV_DIM, HEAD_DIM)
    e256, et256 = _group_indicators(D_MODEL, MEM_HEAD_DIM)
    gq = _row(jnp.tile(p["g_q"], MOBA_HEADS)) * HEAD_DIM ** -0.5
    gk = _row(jnp.tile(p["g_k"], MOBA_KV_HEADS))
    gmq = _row(jnp.tile(p["g_mq"], MEM_HEADS)) * MEM_HEAD_DIM ** -0.5
    inproj = [_row(p["g_mix_norm"]), wz, wxbc, wdt, wq, wk, wv, wqm, wg, _row(p["b_gate"]),
              gq, gk, gmq, e64, et64, e64k, et64k, e256, et256]

    pad_h = lambda v: jnp.pad(_row(v), ((0, 0), (0, LANES - SSM_HEADS)))
    e2 = np.zeros((LANES, D_INNER), np.float32)
    e2[np.arange(D_INNER) // SSM_HEAD_DIM, np.arange(D_INNER)] = 1.0
    e2 = jnp.asarray(np.concatenate([e2, e2], axis=0), BF16)
    eg, etg2 = _group_indicators(D_INNER, D_INNER // SSM_GROUPS)
    ssd = [p["conv_w"].astype(F32), _row(p["conv_b"]), pad_h(p["dt_bias"]), pad_h(p["a_log"]),
           _row(jnp.repeat(p["d_skip"], SSM_HEAD_DIM)), _row(p["g_ssm_norm"]), e2, eg, etg2]

    memkv = [_row(p["g_mem_norm"]), p["w_mem_kv"].astype(BF16),
             _row(jnp.tile(p["g_mk"], MEM_HEADS)), e256, et256]
    merge = [p["w_ssm_out"].astype(BF16), p["w_moba_out"].astype(BF16), p["w_mem_out"].astype(BF16),
             p["w_out"].astype(BF16), _row(p["g_ffn_norm"])]
    keys = p["peer_keys"].astype(BF16).reshape(PEER_HEADS * 2, PEER_KEYS, PEER_HALF)
    peer_topk = [p["w_pq"].astype(BF16), jnp.concatenate([keys, keys], axis=-1)]
    return {"inproj": inproj, "ssd": ssd, "memkv": memkv, "merge": merge, "peer_topk": peer_topk,
            "peer_u": _pack_table(p["peer_u"]), "peer_v": _pack_table(p["peer_v"])}


def _tile(n, pref):
    return pref if n % pref == 0 else n


def _group_forward(x, wts, ssd_state, attn_fn, mem_k, mem_v):
    b, t, _ = x.shape
    n = b * t
    x2 = x.reshape(n, D_MODEL)
    z, xbc, dt, q, k, v, qm, gate = _inproj(x2, wts, _tile(n, 256))
    conv0, ssm0, q_rows, n_valid = ssd_state
    y_ssd, ssm_new, conv_new = _ssd(xbc.reshape(b, t, CONV_DIM), z.reshape(b, t, D_INNER),
                                    dt.reshape(b, t, LANES), conv0, ssm0, wts, q_rows, n_valid)
    k3, v3 = k.reshape(b, t, KV_DIM), v.reshape(b, t, KV_DIM)
    y_moba = attn_fn(q.reshape(b, t, D_MODEL), k3, v3)
    y_mem = _memattn(qm.reshape(b, t, D_MODEL), mem_k, mem_v, _tile(t, 512))
    x1, hn = _merge(x2, y_ssd.reshape(n, D_INNER), y_moba.reshape(n, D_MODEL),
                    y_mem.reshape(n, D_MODEL), gate, wts, _tile(n, 256))
    out = _peer(x1, hn, wts, _tile(n, 256), _tile(n, 128))
    return (out.reshape(b, t, D_MODEL),
            ssm_new.reshape(b, SSM_HEADS, SSM_HEAD_DIM, D_STATE), conv_new,
            k3.reshape(b, t, MOBA_KV_HEADS, HEAD_DIM), v3.reshape(b, t, MOBA_KV_HEADS, HEAD_DIM))


def kernel(x_prompt, x_sample, cache_k, cache_v, state_ssm, state_conv, cache_mem_k, cache_mem_v,
           page_table, mem_prompt, g_mix_norm, w_in, conv_w, conv_b, dt_bias, a_log, d_skip,
           g_ssm_norm, w_ssm_out, g_q, g_k, rel_bias, w_moba_out, g_mem_norm, w_mem_kv, g_mq, g_mk,
           w_mem_out, b_gate, w_out, g_ffn_norm, w_pq, peer_keys, peer_u, peer_v):
    depth = w_in.shape[0]
    bp, tp, _ = x_prompt.shape
    sb, dec_seq, _ = x_sample.shape
    mem_len = mem_prompt.shape[1]
    n_pool, page = cache_k.shape[1], cache_k.shape[2]
    past_len = page_table.shape[1] * page
    assert past_len % MOBA_BLOCK == 0 and tp % MOBA_BLOCK == 0

    far_d = _far_distance()
    n_near_p = -(-(far_d + MOBA_BLOCK - 1) // MOBA_BLOCK) - 1
    bias_p = _bias_prompt(rel_bias, 2 * n_near_p + 2)
    farb_p = _row(jnp.repeat(rel_bias[NUM_BUCKETS - 1], Q_BLOCK))
    nbp = past_len // MOBA_BLOCK
    n_far_s = min(max((past_len - (MOBA_BLOCK - 1) - far_d) // MOBA_BLOCK + 1, 0), nbp)
    bias_s = _bias_sample(rel_bias, past_len, n_far_s, nbp - n_far_s, dec_seq)
    farb_s = _row(jnp.repeat(rel_bias[NUM_BUCKETS - 1], dec_seq))

    per_layer = dict(g_mix_norm=g_mix_norm, w_in=w_in, conv_w=conv_w, conv_b=conv_b, dt_bias=dt_bias,
                     a_log=a_log, d_skip=d_skip, g_ssm_norm=g_ssm_norm, w_ssm_out=w_ssm_out, g_q=g_q,
                     g_k=g_k, w_moba_out=w_moba_out, g_mem_norm=g_mem_norm, w_mem_kv=w_mem_kv,
                     g_mq=g_mq, g_mk=g_mk, w_mem_out=w_mem_out, b_gate=b_gate, w_out=w_out,
                     g_ffn_norm=g_ffn_norm, w_pq=w_pq, peer_keys=peer_keys, peer_u=peer_u,
                     peer_v=peer_v)
    xp, xs = x_prompt, x_sample
    outs = [[] for _ in range(10)]
    chunk = math.gcd(tp, SSD_CHUNK)
    for l in range(depth):
        wts = _prepare({name: val[l] for name, val in per_layer.items()})
        mk, mv = _memkv(mem_prompt.reshape(bp * mem_len, D_MODEL), wts)
        mk, mv = mk.reshape(bp, mem_len, D_MODEL), mv.reshape(bp, mem_len, D_MODEL)
        prompt_attn = functools.partial(_moba_prompt, bias_tiles=bias_p, farb=farb_p, n_near=n_near_p)
        xp, h_p, c_p, k_p, v_p = _group_forward(xp, wts, (None, None, chunk, chunk), prompt_attn, mk, mv)
        sample_attn = functools.partial(
            _moba_sample, cache_k=cache_k[l].reshape(n_pool, page, KV_DIM),
            cache_v=cache_v[l].reshape(n_pool, page, KV_DIM), page_table=page_table,
            bias_tiles=bias_s, farb=farb_s, n_far=n_far_s, pages_per_step=8)
        conv0 = jnp.pad(state_conv[l], ((0, 0), (SUBLANES - (CONV_WIDTH - 1), 0), (0, 0)))
        ssm0 = state_ssm[l].reshape(sb, D_INNER, D_STATE)
        xs, h_s, c_s, k_s, v_s = _group_forward(
            xs, wts, (conv0, ssm0, Q_BLOCK, dec_seq), sample_attn,
            cache_mem_k[l].reshape(sb, mem_len, D_MODEL), cache_mem_v[l].reshape(sb, mem_len, D_MODEL))
        heads = (MEM_HEADS, MEM_HEAD_DIM)
        for lst, val in zip(outs, (k_p, v_p, k_s, v_s, h_p, h_s, c_p, c_s,
                                   mk.reshape(bp, mem_len, *heads), mv.reshape(bp, mem_len, *heads))):
            lst.append(val)
    return (xp, xs) + tuple(jnp.stack(lst) for lst in outs)
```

```python
import functools
import math

import numpy as np
import jax
import jax.numpy as jnp
from jax import lax
from jax.experimental import pallas as pl
from jax.experimental.pallas import tpu as pltpu

F32 = jnp.float32
BF16 = jnp.bfloat16
I32 = jnp.int32
NEG_INF = float("-inf")

D_MODEL = 1024
D_INNER = 2048
SSM_HEAD_DIM = 64
SSM_HEADS = 32
SSM_GROUPS = 4
D_STATE = 128
CONV_WIDTH = 4
CONV_DIM = D_INNER + 2 * SSM_GROUPS * D_STATE
SSD_CHUNK = 256
HEAD_DIM = 64
MOBA_HEADS = 16
MOBA_KV_HEADS = 4
Q_PER_KV = 4
KV_DIM = MOBA_KV_HEADS * HEAD_DIM
MOBA_BLOCK = 256
MOBA_TOPK = 3
Q_BLOCK = 128
NUM_BUCKETS = 32
MAX_DISTANCE = 2048
MEM_HEADS = 4
MEM_HEAD_DIM = 256
PEER_HEADS = 8
PEER_KEYS = 128
PEER_HALF = 128
PEER_TOPK = 16
N_BRANCHES = 3
EPS = 1e-6
LOG2E = math.log2(math.e)

LANES = 128
SUBLANES = 8
VMEM_LIMIT_BYTES = 56 * 1024 * 1024


def _cparams(*sem):
    return pltpu.CompilerParams(dimension_semantics=sem, vmem_limit_bytes=VMEM_LIMIT_BYTES)


def _vmem_full():
    return pl.BlockSpec(memory_space=pltpu.VMEM)


def _dot(a, b):
    return jnp.dot(a, b, preferred_element_type=F32)


def _split_bf16(x):
    hi = x.astype(BF16)
    lo = (x - hi.astype(F32)).astype(BF16)
    return hi, lo


def _group_indicators(dim, gsize):
    e = np.zeros((dim, LANES), np.float32)
    e[np.arange(dim), np.arange(dim) // gsize] = 1.0
    et2 = np.concatenate([e.T, e.T], axis=0)
    return jnp.asarray(e, BF16), jnp.asarray(et2, BF16)


def _group_rr(x, e_ref, et2_ref, gsize):
    ssq = _dot((x * x).astype(BF16), e_ref[...])
    r = lax.rsqrt(ssq * (1.0 / gsize) + EPS)
    r_hi, r_lo = _split_bf16(r)
    return _dot(jnp.concatenate([r_hi, r_lo], axis=1), et2_ref[...])


def _inproj_kernel(x_ref, gmix_ref, wz_ref, wxbc_ref, wdt_ref, wq_ref, wk_ref, wv_ref, wqm_ref,
                   wg_ref, bgate_ref, gq_ref, gk_ref, gmq_ref, e64_ref, et64_ref, e64k_ref,
                   et64k_ref, e256_ref, et256_ref,
                   z_ref, xbc_ref, dt_ref, q_ref, k_ref, v_ref, qm_ref, gate_ref):
    x = x_ref[...]
    h = (x * lax.rsqrt(jnp.mean(x * x, axis=-1, keepdims=True) + EPS) * gmix_ref[...]).astype(BF16)
    z_ref[...] = _dot(h, wz_ref[...]).astype(BF16)
    xbc_ref[...] = _dot(h, wxbc_ref[...])
    dt_ref[...] = _dot(h, wdt_ref[...])
    q = _dot(h, wq_ref[...])
    q_ref[...] = (q * _group_rr(q, e64_ref, et64_ref, HEAD_DIM) * gq_ref[...]).astype(BF16)
    k = _dot(h, wk_ref[...])
    k_ref[...] = k * _group_rr(k, e64k_ref, et64k_ref, HEAD_DIM) * gk_ref[...]
    v_ref[...] = _dot(h, wv_ref[...])
    qm = _dot(h, wqm_ref[...])
    qm_ref[...] = (qm * _group_rr(qm, e256_ref, et256_ref, MEM_HEAD_DIM) * gmq_ref[...]).astype(BF16)
    gate_ref[...] = jax.nn.sigmoid(_dot(h, wg_ref[...]) + bgate_ref[...]).astype(BF16)


def _inproj(x, wts, tm):
    n = x.shape[0]
    assert n % tm == 0
    consts = wts["inproj"]
    row = lambda w: pl.BlockSpec((tm, w), lambda i: (i, 0))
    out_shape = (
        jax.ShapeDtypeStruct((n, D_INNER), BF16),
        jax.ShapeDtypeStruct((n, CONV_DIM), F32),
        jax.ShapeDtypeStruct((n, LANES), F32),
        jax.ShapeDtypeStruct((n, D_MODEL), BF16),
        jax.ShapeDtypeStruct((n, KV_DIM), F32),
        jax.ShapeDtypeStruct((n, KV_DIM), F32),
        jax.ShapeDtypeStruct((n, D_MODEL), BF16),
        jax.ShapeDtypeStruct((n, N_BRANCHES * D_MODEL), BF16),
    )
    return pl.pallas_call(
        _inproj_kernel,
        grid=(n // tm,),
        in_specs=[row(D_MODEL)] + [_vmem_full()] * len(consts),
        out_specs=tuple(row(s.shape[1]) for s in out_shape),
        out_shape=out_shape,
        compiler_params=_cparams("parallel"),
        name="inproj",
    )(x, *consts)


def _ssd_kernel(*refs, q_rows, n_valid, has_init):
    if has_init:
        (xbc_ref, z_ref, dt_ref, conv0_ref, ssm0_ref, cw_ref, cb_ref, dtb_ref, alog_ref, dskip_ref,
         gssm_ref, e2_ref, eg_ref, etg2_ref, y_ref, ssm_out_ref, conv_out_ref,
         xw_ref, st_ref, yb_ref) = refs
    else:
        (xbc_ref, z_ref, dt_ref, cw_ref, cb_ref, dtb_ref, alog_ref, dskip_ref,
         gssm_ref, e2_ref, eg_ref, etg2_ref, y_ref, ssm_out_ref, conv_out_ref,
         xw_ref, st_ref, yb_ref) = refs
    c = pl.program_id(1)
    last = pl.num_programs(1) - 1
    Q, NV = q_rows, n_valid
    G = SSM_GROUPS
    GW = D_INNER // G

    @pl.when(c == 0)
    def _init():
        if has_init:
            xw_ref[0:SUBLANES, :] = conv0_ref[0]
            st_ref[...] = ssm0_ref[0].T
        else:
            xw_ref[0:SUBLANES, :] = jnp.zeros((SUBLANES, CONV_DIM), F32)
            st_ref[...] = jnp.zeros_like(st_ref)

    xw_ref[SUBLANES:SUBLANES + NV, :] = xbc_ref[0]
    if NV < Q:
        xw_ref[SUBLANES + NV:SUBLANES + Q, :] = jnp.zeros((Q - NV, CONV_DIM), F32)

    acc = jnp.broadcast_to(cb_ref[...], (Q, CONV_DIM))
    for kk in range(CONV_WIDTH):
        off = SUBLANES - (CONV_WIDTH - 1) + kk
        acc = acc + cw_ref[kk:kk + 1, :] * xw_ref[off:off + Q, :]
    xc = jax.nn.silu(acc)
    xs = xc[:, :D_INNER]
    bm = xc[:, D_INNER:D_INNER + G * D_STATE]
    cm = xc[:, D_INNER + G * D_STATE:]

    dtv = dt_ref[0]
    if NV < Q:
        dtv = jnp.concatenate([dtv, jnp.zeros((Q - NV, LANES), F32)], axis=0)
    xdt_pre = dtv + dtb_ref[...]
    dt = jnp.maximum(xdt_pre, 0.0) + jnp.log1p(jnp.exp(-jnp.abs(xdt_pre)))
    row_i = lax.broadcasted_iota(I32, (Q, LANES), 0)
    if NV < Q:
        dt = jnp.where(row_i < NV, dt, 0.0)
    a = -jnp.exp(alog_ref[...])
    da = dt * a

    ri = lax.broadcasted_iota(I32, (Q, Q), 0)
    ci = lax.broadcasted_iota(I32, (Q, Q), 1)
    causal = ri >= ci
    lower = causal.astype(F32)
    upper = (ri <= ci).astype(F32)
    cum = jnp.dot(lower, da, preferred_element_type=F32, precision=lax.Precision.HIGHEST)
    cum_t = jnp.dot(da.T, upper, preferred_element_type=F32, precision=lax.Precision.HIGHEST)

    def expand(v):
        hi, lo = _split_bf16(v)
        return _dot(jnp.concatenate([hi, lo], axis=1), e2_ref[...])

    dt_x = expand(dt)
    ecum_x = expand(jnp.exp(cum))
    toend_x = expand(jnp.exp(cum[Q - 1:Q, :] - cum))
    xdt = xs * dt_x
    xdt_b = xdt.astype(BF16)
    xw_b = (xdt * toend_x).astype(BF16)
    dec_row = ecum_x[Q - 1:Q, :]
    lane_lo = lax.broadcasted_iota(I32, (Q, LANES), 1) < SSM_HEAD_DIM

    for g in range(G):
        cg = cm[:, g * D_STATE:(g + 1) * D_STATE].astype(BF16)
        bg = bm[:, g * D_STATE:(g + 1) * D_STATE]
        bg_b = bg.astype(BF16)
        cb = lax.dot_general(cg, bg_b, (((1,), (1,)), ((), ())), preferred_element_type=F32)
        st_g = st_ref[:, g * GW:(g + 1) * GW]
        y_off = _dot(cg, st_g.astype(BF16))
        for pair in range(GW // LANES):
            col0 = g * GW + pair * LANES
            x2 = xdt_b[:, col0:col0 + LANES]
            halves = []
            for e in range(2):
                hh = col0 // SSM_HEAD_DIM + e
                seg = cum[:, hh:hh + 1] - cum_t[hh:hh + 1, :]
                lmat = jnp.exp(jnp.where(causal, seg, NEG_INF))
                halves.append(_dot((cb * lmat).astype(BF16), x2))
            yb_ref[:, col0:col0 + LANES] = jnp.where(lane_lo, halves[0], halves[1])
        yb_ref[:, g * GW:(g + 1) * GW] = (yb_ref[:, g * GW:(g + 1) * GW]
                                          + y_off * ecum_x[:, g * GW:(g + 1) * GW])
        st_ref[:, g * GW:(g + 1) * GW] = (st_g * dec_row[:, g * GW:(g + 1) * GW]
                                          + _dot(bg.T.astype(BF16), xw_b[:, g * GW:(g + 1) * GW]))

    y = yb_ref[...] + xs * dskip_ref[...]
    zf = z_ref[0].astype(F32)
    if NV < Q:
        zf = jnp.concatenate([zf, jnp.zeros((Q - NV, D_INNER), F32)], axis=0)
    yz = y * jax.nn.silu(zf)
    yn = yz * _group_rr(yz, eg_ref, etg2_ref, GW) * gssm_ref[...]
    y_ref[0] = yn[0:NV, :].astype(BF16)

    xw_ref[0:SUBLANES, :] = xw_ref[NV:NV + SUBLANES, :]

    @pl.when(c == last)
    def _fin():
        ssm_out_ref[0] = st_ref[...].T
        conv_out_ref[0] = xw_ref[SUBLANES - (CONV_WIDTH - 1):SUBLANES, :]


def _ssd(xbc, z, dt, conv0, ssm0, wts, q_rows, n_valid):
    b, t, _ = xbc.shape
    assert t % n_valid == 0 and n_valid >= CONV_WIDTH - 1
    nc = t // n_valid
    has_init = conv0 is not None
    consts = wts["ssd"]
    tile = lambda w: pl.BlockSpec((1, n_valid, w), lambda i, c: (i, c, 0))
    in_specs = [tile(CONV_DIM), tile(D_INNER), tile(LANES)]
    args = [xbc, z, dt]
    if has_init:
        in_specs += [pl.BlockSpec((1, SUBLANES, CONV_DIM), lambda i, c: (i, 0, 0)),
                     pl.BlockSpec((1, D_INNER, D_STATE), lambda i, c: (i, 0, 0))]
        args += [conv0, ssm0]
    in_specs += [_vmem_full()] * len(consts)
    out_shape = (jax.ShapeDtypeStruct((b, t, D_INNER), BF16),
                 jax.ShapeDtypeStruct((b, D_INNER, D_STATE), F32),
                 jax.ShapeDtypeStruct((b, CONV_WIDTH - 1, CONV_DIM), F32))
    out_specs = (tile(D_INNER),
                 pl.BlockSpec((1, D_INNER, D_STATE), lambda i, c: (i, 0, 0)),
                 pl.BlockSpec((1, CONV_WIDTH - 1, CONV_DIM), lambda i, c: (i, 0, 0)))
    return pl.pallas_call(
        functools.partial(_ssd_kernel, q_rows=q_rows, n_valid=n_valid, has_init=has_init),
        grid=(b, nc),
        in_specs=in_specs,
        out_specs=out_specs,
        out_shape=out_shape,
        scratch_shapes=[pltpu.VMEM((SUBLANES + q_rows, CONV_DIM), F32),
                        pltpu.VMEM((D_STATE, D_INNER), F32),
                        pltpu.VMEM((q_rows, D_INNER), F32)],
        compiler_params=_cparams("parallel", "arbitrary"),
        name="ssd_scan",
    )(*args, *consts)


def _rel_bucket(dist):
    n = jnp.maximum(dist, 0)
    max_exact = NUM_BUCKETS // 2
    nf = jnp.maximum(n, 1).astype(F32)
    large = max_exact + (jnp.log(nf / max_exact) / math.log(MAX_DISTANCE / max_exact)
                         * (NUM_BUCKETS - max_exact)).astype(I32)
    large = jnp.minimum(large, NUM_BUCKETS - 1)
    return jnp.where(n < max_exact, n, large)


def _far_distance():
    d = np.arange(1, 4 * MAX_DISTANCE, dtype=np.float64)
    large = 16 + np.floor(np.log(d / 16) / math.log(MAX_DISTANCE / 16) * 16)
    below = np.nonzero(large < NUM_BUCKETS - 1)[0]
    return int(d[below[-1]]) + 1 + 1


def _bias_from_dist(dist, rbl_ref):
    bucket = _rel_bucket(dist)
    val = jnp.zeros(dist.shape, F32)
    for b in range(NUM_BUCKETS):
        val = jnp.where(bucket == b, rbl_ref[b:b + 1, :], val)
    return jnp.where(dist >= 0, val * LOG2E, NEG_INF)


def _bias_prompt_kernel(rbl_ref, o_ref):
    di = pl.program_id(0)
    w = Q_PER_KV * Q_BLOCK
    j = lax.broadcasted_iota(I32, (MOBA_BLOCK, w), 0)
    i = lax.broadcasted_iota(I32, (MOBA_BLOCK, w), 1) % Q_BLOCK
    o_ref[0, 0] = _bias_from_dist(di * Q_BLOCK + i - j, rbl_ref.at[0]).astype(BF16)


def _bias_prompt(rel_bias, n_tiles):
    w = Q_PER_KV * Q_BLOCK
    rbl = jnp.repeat(rel_bias.reshape(NUM_BUCKETS, MOBA_KV_HEADS, Q_PER_KV).transpose(1, 0, 2),
                     Q_BLOCK, axis=2)
    return pl.pallas_call(
        _bias_prompt_kernel,
        grid=(n_tiles, MOBA_KV_HEADS),
        in_specs=[pl.BlockSpec((1, NUM_BUCKETS, w), lambda d, h: (h, 0, 0))],
        out_specs=pl.BlockSpec((1, 1, MOBA_BLOCK, w), lambda d, h: (d, h, 0, 0)),
        out_shape=jax.ShapeDtypeStruct((n_tiles, MOBA_KV_HEADS, MOBA_BLOCK, w), BF16),
        compiler_params=_cparams("parallel", "parallel"),
        name="moba_bias_prompt",
    )(rbl)


def _bias_sample_kernel(rbl_ref, o_ref, *, past_len, first_block, n_near, dec_seq):
    r = pl.program_id(0)
    kpos0 = jnp.where(r < n_near, (first_block + r) * MOBA_BLOCK, past_len)
    j = lax.broadcasted_iota(I32, (MOBA_BLOCK, LANES), 0)
    t = lax.broadcasted_iota(I32, (MOBA_BLOCK, LANES), 1) % dec_seq
    o_ref[0] = _bias_from_dist(past_len + t - (kpos0 + j), rbl_ref)


def _bias_sample(rel_bias, past_len, first_block, n_near, dec_seq):
    rbl = jnp.repeat(rel_bias, dec_seq, axis=1)
    return pl.pallas_call(
        functools.partial(_bias_sample_kernel, past_len=past_len, first_block=first_block,
                          n_near=n_near, dec_seq=dec_seq),
        grid=(n_near + 1,),
        in_specs=[_vmem_full()],
        out_specs=pl.BlockSpec((1, MOBA_BLOCK, LANES), lambda r: (r, 0, 0)),
        out_shape=jax.ShapeDtypeStruct((n_near + 1, MOBA_BLOCK, LANES), F32),
        compiler_params=_cparams("parallel"),
        name="moba_bias_sample",
    )(rbl)


def _kmean_kernel(k_ref, o_ref, *, nb):
    for b in range(nb):
        blk = k_ref[0, b * MOBA_BLOCK:(b + 1) * MOBA_BLOCK, :]
        o_ref[0, b:b + 1, :] = jnp.sum(blk, axis=0, keepdims=True) * (1.0 / MOBA_BLOCK)


def _kmean(k):
    b, t, _ = k.shape
    nb = t // MOBA_BLOCK
    return pl.pallas_call(
        functools.partial(_kmean_kernel, nb=nb),
        grid=(b,),
        in_specs=[pl.BlockSpec((1, t, KV_DIM), lambda i: (i, 0, 0))],
        out_specs=pl.BlockSpec((1, nb, KV_DIM), lambda i: (i, 0, 0)),
        out_shape=jax.ShapeDtypeStruct((b, nb, KV_DIM), F32),
        compiler_params=_cparams("parallel"),
        name="moba_kmean",
    )(k)


def _select_blocks(gate, own, n_sel_rows):
    nb = gate.shape[0]
    blk = lax.broadcasted_iota(I32, gate.shape, 0)
    sel = jnp.zeros(gate.shape, F32)
    for t in range(MOBA_TOPK):
        m = jnp.max(gate, axis=0, keepdims=True)
        pos = jnp.min(jnp.where(gate == m, blk, nb), axis=0, keepdims=True)
        hit = blk == pos
        sel = jnp.where(hit, jnp.maximum(sel, jnp.where(t < n_sel_rows, 1.0, 0.0)), sel)
        gate = jnp.where(hit, NEG_INF, gate)
    return sel, blk


def _moba_prompt_kernel(qt_ref, k_ref, vt_ref, km_ref, bias_ref, farb_ref, o_ref,
                        qx_ref, add_ref, m_ref, l_ref, acc_ref, s_ref, *, n_near):
    a = pl.program_id(1)
    own = a // 2
    par = a % 2
    nb = k_ref.shape[1]
    hw = Q_PER_KV * Q_BLOCK
    w = MOBA_KV_HEADS * hw

    row_h = lax.broadcasted_iota(I32, (KV_DIM, Q_BLOCK), 0) // HEAD_DIM
    for h in range(MOBA_KV_HEADS):
        for g in range(Q_PER_KV):
            piece = jnp.where(row_h == h, qt_ref[0, g], jnp.zeros((), BF16))
            col = (h * Q_PER_KV + g) * Q_BLOCK
            qx_ref[:, col:col + Q_BLOCK] = piece

    km_hi, km_lo = _split_bf16(km_ref[0])
    gate = _dot(km_hi, qx_ref[...]) + _dot(km_lo, qx_ref[...])
    blk0 = lax.broadcasted_iota(I32, (nb, w), 0)
    gate = jnp.where(blk0 < own, gate, NEG_INF)
    sel, blk = _select_blocks(gate, own, own)
    far = (own - blk) > n_near
    add = jnp.where(sel > 0.0, jnp.where(far, farb_ref[...], 0.0), NEG_INF)
    add_ref[...] = jnp.where(blk == own, 0.0, add)

    m_ref[...] = jnp.full(m_ref.shape, NEG_INF, F32)
    l_ref[...] = jnp.zeros(l_ref.shape, F32)
    acc_ref[...] = jnp.zeros(acc_ref.shape, F32)

    def step(b, di):
        kb = k_ref[0, b]
        for h in range(MOBA_KV_HEADS):
            s_ref[h] = _dot(kb, qx_ref[:, h * hw:(h + 1) * hw])
        for h in range(MOBA_KV_HEADS):
            s = s_ref[h]
            if di is not None:
                s = s + bias_ref[di, h].astype(F32)
            add = add_ref[pl.ds(b, 1), h * hw:(h + 1) * hw]
            m_prev = m_ref[h]
            m_new = jnp.maximum(m_prev, jnp.max(s, axis=0, keepdims=True) + add)
            p = jnp.exp2(s - (m_new - add))
            alpha = jnp.exp2(m_prev - m_new)
            l_ref[h] = alpha * l_ref[h] + jnp.sum(p, axis=0, keepdims=True)
            vth = vt_ref[0, b, h * HEAD_DIM:(h + 1) * HEAD_DIM, :]
            acc_ref[h] = alpha * acc_ref[h] + _dot(vth, p.astype(BF16))
            m_ref[h] = m_new

    step(own, par)

    def near_body(kk, carry):
        step(own - kk, par + 2 * kk)
        return carry

    lax.fori_loop(1, jnp.minimum(n_near, own) + 1, near_body, 0)

    def far_body(b, carry):
        step(b, None)
        return carry

    lax.fori_loop(0, jnp.maximum(own - n_near, 0), far_body, 0)

    for h in range(MOBA_KV_HEADS):
        o_ref[0, h] = (acc_ref[h] / l_ref[h]).astype(BF16)


def _moba_prompt(q, k, v, bias_tiles, farb, n_near):
    b, t, _ = q.shape
    assert t % MOBA_BLOCK == 0
    nb, nq = t // MOBA_BLOCK, t // Q_BLOCK
    hw = Q_PER_KV * Q_BLOCK
    w = MOBA_KV_HEADS * hw
    km = _kmean(k)
    qt = q.reshape(b, t, MOBA_KV_HEADS, Q_PER_KV, HEAD_DIM).transpose(0, 3, 2, 4, 1).reshape(
        b, Q_PER_KV, KV_DIM, t)
    kb = k.astype(BF16).reshape(b, nb, MOBA_BLOCK, KV_DIM)
    vt = v.astype(BF16).reshape(b, nb, MOBA_BLOCK, KV_DIM).transpose(0, 1, 3, 2)
    out = pl.pallas_call(
        functools.partial(_moba_prompt_kernel, n_near=n_near),
        grid=(b, nq),
        in_specs=[pl.BlockSpec((1, Q_PER_KV, KV_DIM, Q_BLOCK), lambda i, a: (i, 0, 0, a)),
                  pl.BlockSpec((1, nb, MOBA_BLOCK, KV_DIM), lambda i, a: (i, 0, 0, 0)),
                  pl.BlockSpec((1, nb, KV_DIM, MOBA_BLOCK), lambda i, a: (i, 0, 0, 0)),
                  pl.BlockSpec((1, nb, KV_DIM), lambda i, a: (i, 0, 0)),
                  _vmem_full(), _vmem_full()],
        out_specs=pl.BlockSpec((1, MOBA_KV_HEADS, HEAD_DIM, hw), lambda i, a: (i, 0, 0, a)),
        out_shape=jax.ShapeDtypeStruct((b, MOBA_KV_HEADS, HEAD_DIM, nq * hw), BF16),
        scratch_shapes=[pltpu.VMEM((KV_DIM, w), BF16),
                        pltpu.VMEM((nb, w), F32),
                        pltpu.VMEM((MOBA_KV_HEADS, 1, hw), F32),
                        pltpu.VMEM((MOBA_KV_HEADS, 1, hw), F32),
                        pltpu.VMEM((MOBA_KV_HEADS, HEAD_DIM, hw), F32),
                        pltpu.VMEM((MOBA_KV_HEADS, MOBA_BLOCK, hw), F32)],
        compiler_params=_cparams("parallel", "arbitrary"),
        name="moba_prompt",
    )(qt, kb, vt, km, bias_tiles, farb)
    out = out.reshape(b, MOBA_KV_HEADS, HEAD_DIM, nq, Q_PER_KV, Q_BLOCK).transpose(0, 3, 5, 1, 4, 2)
    return out.reshape(b, t, D_MODEL)


def _moba_sample_kernel(pt_ref, qx_ref, *rest, pages_per_step, n_far, dec_seq):
    del pt_ref
    gp = pages_per_step
    k_refs, v_refs = rest[:gp], rest[gp:2 * gp]
    (knew_ref, vnew_ref, biass_ref, farb_ref, o_ref,
     st_ref, v_ref, ks_ref, add_ref) = rest[2 * gp:]
    tn = (((0,), (0,)), ((), ()))
    j = pl.program_id(1)
    nbp = st_ref.shape[0]
    n_near = nbp - n_far
    page = MOBA_BLOCK // 2
    qx = qx_ref[0]

    for i in range(gp):
        blk = j * (gp // 2) + i // 2
        half = i % 2
        rows = slice(half * page, (half + 1) * page)
        heads = range(MOBA_KV_HEADS)
        kp = jnp.concatenate([k_refs[i][0, 0, :, h, :] for h in heads], axis=1)
        vp = jnp.concatenate([v_refs[i][0, 0, :, h, :] for h in heads], axis=1)
        st_ref[blk, rows, :] = _dot(kp.astype(BF16), qx)
        v_ref[blk, rows, :] = vp.astype(BF16)
        ksum = jnp.sum(kp, axis=0, keepdims=True)
        if half == 0:
            ks_ref[pl.ds(blk, 1), :] = ksum
        else:
            ks_ref[pl.ds(blk, 1), :] = ks_ref[pl.ds(blk, 1), :] + ksum

    @pl.when(j == pl.num_programs(1) - 1)
    def _fin():
        km_hi, km_lo = _split_bf16(ks_ref[...] * (1.0 / MOBA_BLOCK))
        gate = _dot(km_hi, qx) + _dot(km_lo, qx)
        sel, blk = _select_blocks(gate, nbp, nbp)
        add_ref[...] = jnp.where(sel > 0.0, jnp.where(blk < n_far, farb_ref[...], 0.0), NEG_INF)

        s_own = _dot(knew_ref[0].astype(BF16), qx) + biass_ref[n_near, 0:Q_BLOCK, :]
        m = jnp.max(s_own, axis=0, keepdims=True)

        def far_logits(b, m):
            lg = st_ref[b] + add_ref[pl.ds(b, 1), :]
            st_ref[b] = lg
            return jnp.maximum(m, jnp.max(lg, axis=0, keepdims=True))

        m = lax.fori_loop(0, n_far, far_logits, m)
        for r in range(n_near):
            b = n_far + r
            lg = st_ref[b] + add_ref[b:b + 1, :] + biass_ref[r]
            st_ref[b] = lg
            m = jnp.maximum(m, jnp.max(lg, axis=0, keepdims=True))

        p_own = jnp.exp2(s_own - m)
        l0 = jnp.sum(p_own, axis=0, keepdims=True)
        acc0 = lax.dot_general(vnew_ref[0].astype(BF16), p_own.astype(BF16), tn,
                               preferred_element_type=F32)

        def weighted(b, carry):
            l, acc = carry
            p = jnp.exp2(st_ref[b] - m)
            pv = lax.dot_general(v_ref[b], p.astype(BF16), tn, preferred_element_type=F32)
            return l + jnp.sum(p, axis=0, keepdims=True), acc + pv

        l, acc = lax.fori_loop(0, nbp, weighted, (l0, acc0))
        y = acc / l
        lane_h = lax.broadcasted_iota(I32, (HEAD_DIM, LANES), 1) // (Q_PER_KV * dec_seq)
        out = jnp.zeros((HEAD_DIM, LANES), F32)
        for h in range(MOBA_KV_HEADS):
            out = jnp.where(lane_h == h, y[h * HEAD_DIM:(h + 1) * HEAD_DIM, :], out)
        o_ref[0] = out


def _moba_sample(q, k_new, v_new, cache_k, cache_v, layer, page_table, bias_tiles, farb, n_far,
                 pages_per_step):
    s, dec_seq, _ = q.shape
    page = cache_k.shape[2]
    n_pages = page_table.shape[1]
    assert page * 2 == MOBA_BLOCK and n_pages % pages_per_step == 0 and pages_per_step % 2 == 0
    assert MOBA_HEADS * dec_seq == LANES and dec_seq <= Q_BLOCK
    nbp = n_pages // 2
    gp = pages_per_step
    q5 = q.reshape(s, dec_seq, MOBA_KV_HEADS, Q_PER_KV, HEAD_DIM)
    qx = jnp.einsum("sthgd,ph->spdhgt", q5, jnp.eye(MOBA_KV_HEADS, dtype=q.dtype)).reshape(
        s, KV_DIM, LANES)
    pad = ((0, 0), (0, Q_BLOCK - dec_seq), (0, 0))
    k_pad, v_pad = jnp.pad(k_new, pad), jnp.pad(v_new, pad)

    def page_spec(i):
        return pl.BlockSpec((1, 1, page, MOBA_KV_HEADS, HEAD_DIM),
                            lambda b, j, pt: (layer, pt[b, j * gp + i], 0, 0, 0))

    seq_spec = lambda r, w: pl.BlockSpec((1, r, w), lambda b, j, pt: (b, 0, 0))
    grid_spec = pltpu.PrefetchScalarGridSpec(
        num_scalar_prefetch=1,
        grid=(s, n_pages // gp),
        in_specs=([seq_spec(KV_DIM, LANES)] + [page_spec(i) for i in range(gp)] * 2
                  + [seq_spec(Q_BLOCK, KV_DIM), seq_spec(Q_BLOCK, KV_DIM), _vmem_full(), _vmem_full()]),
        out_specs=seq_spec(HEAD_DIM, LANES),
        scratch_shapes=[pltpu.VMEM((nbp, MOBA_BLOCK, LANES), F32),
                        pltpu.VMEM((nbp, MOBA_BLOCK, KV_DIM), BF16),
                        pltpu.VMEM((nbp, KV_DIM), F32),
                        pltpu.VMEM((nbp, LANES), F32)],
    )
    out = pl.pallas_call(
        functools.partial(_moba_sample_kernel, pages_per_step=gp, n_far=n_far, dec_seq=dec_seq),
        grid_spec=grid_spec,
        out_shape=jax.ShapeDtypeStruct((s, HEAD_DIM, LANES), F32),
        compiler_params=_cparams("parallel", "arbitrary"),
        name="moba_sample",
    )(page_table, qx, *([cache_k] * gp), *([cache_v] * gp), k_pad, v_pad, bias_tiles, farb)
    out = out.reshape(s, HEAD_DIM, MOBA_KV_HEADS, Q_PER_KV, dec_seq).transpose(0, 4, 2, 3, 1)
    return out.reshape(s, dec_seq, D_MODEL).astype(BF16)


def _memkv_kernel(mem_ref, gn_ref, w_ref, gk_ref, e_ref, et_ref, k_ref, v_ref):
    x = mem_ref[...]
    h = (x * lax.rsqrt(jnp.mean(x * x, axis=-1, keepdims=True) + EPS) * gn_ref[...]).astype(BF16)
    kv = _dot(h, w_ref[...])
    k = kv[:, :D_MODEL]
    k_ref[...] = k * _group_rr(k, e_ref, et_ref, MEM_HEAD_DIM) * gk_ref[...]
    v_ref[...] = kv[:, D_MODEL:]


def _memkv(mem, wts):
    n = mem.shape[0]
    tm = min(n, 256)
    assert n % tm == 0
    consts = wts["memkv"]
    row = pl.BlockSpec((tm, D_MODEL), lambda i: (i, 0))
    return pl.pallas_call(
        _memkv_kernel,
        grid=(n // tm,),
        in_specs=[row] + [_vmem_full()] * len(consts),
        out_specs=(row, row),
        out_shape=(jax.ShapeDtypeStruct((n, D_MODEL), F32),) * 2,
        compiler_params=_cparams("parallel"),
        name="mem_kv",
    )(mem, *consts)


def _memattn_kernel(q_ref, k_ref, v_ref, o_ref):
    q = q_ref[0]
    for h in range(MEM_HEADS):
        sl = slice(h * MEM_HEAD_DIM, (h + 1) * MEM_HEAD_DIM)
        kh = k_ref[0, :, sl].astype(BF16)
        vh = v_ref[0, :, sl].astype(BF16)
        s = lax.dot_general(q[:, sl], kh, (((1,), (1,)), ((), ())), preferred_element_type=F32)
        p = jnp.exp(s - jnp.max(s, axis=-1, keepdims=True))
        l = jnp.sum(p, axis=-1, keepdims=True)
        o_ref[0, :, sl] = (_dot(p.astype(BF16), vh) / l).astype(BF16)


def _memattn(qm, mk, mv, tq):
    b, t, _ = qm.shape
    m = mk.shape[1]
    assert t % tq == 0
    qspec = pl.BlockSpec((1, tq, D_MODEL), lambda i, a: (i, a, 0))
    mspec = pl.BlockSpec((1, m, D_MODEL), lambda i, a: (i, 0, 0))
    return pl.pallas_call(
        _memattn_kernel,
        grid=(b, t // tq),
        in_specs=[qspec, mspec, mspec],
        out_specs=qspec,
        out_shape=jax.ShapeDtypeStruct((b, t, D_MODEL), BF16),
        compiler_params=_cparams("parallel", "arbitrary"),
        name="mem_attn",
    )(qm, mk, mv)


def _merge_kernel(x_ref, ys_ref, ym_ref, yc_ref, gate_ref, ws_ref, wm_ref, wc_ref, wo_ref, gf_ref,
                  x1_ref, hn_ref):
    g = gate_ref[...].astype(F32)
    merged = (g[:, :D_MODEL] * _dot(ys_ref[...], ws_ref[...])
              + g[:, D_MODEL:2 * D_MODEL] * _dot(ym_ref[...], wm_ref[...])
              + g[:, 2 * D_MODEL:] * _dot(yc_ref[...], wc_ref[...]))
    x1 = x_ref[...] + _dot(merged.astype(BF16), wo_ref[...])
    x1_ref[...] = x1
    hn_ref[...] = x1 * lax.rsqrt(jnp.mean(x1 * x1, axis=-1, keepdims=True) + EPS) * gf_ref[...]


def _merge(x, y_ssd, y_moba, y_mem, gate, wts, tm):
    n = x.shape[0]
    assert n % tm == 0
    consts = wts["merge"]
    row = lambda w: pl.BlockSpec((tm, w), lambda i: (i, 0))
    return pl.pallas_call(
        _merge_kernel,
        grid=(n // tm,),
        in_specs=[row(D_MODEL), row(D_INNER), row(D_MODEL), row(D_MODEL), row(N_BRANCHES * D_MODEL)]
                 + [_vmem_full()] * len(consts),
        out_specs=(row(D_MODEL), row(D_MODEL)),
        out_shape=(jax.ShapeDtypeStruct((n, D_MODEL), F32),) * 2,
        compiler_params=_cparams("parallel"),
        name="merge_out",
    )(x, y_ssd, y_moba, y_mem, gate, *consts)


def _topk_rows(vals, payload, k, v_ref, p_ref):
    r = vals.shape[0]
    rows = lax.broadcasted_iota(I32, vals.shape, 0).astype(F32)
    for i in range(k):
        m = jnp.max(vals, axis=0, keepdims=True)
        pos = jnp.min(jnp.where(vals == m, rows, float(r)), axis=0, keepdims=True)
        hit = rows == pos
        if payload is None:
            p_ref[i:i + 1, :] = pos
        else:
            p_ref[i:i + 1, :] = jnp.max(jnp.where(hit, payload, -1.0), axis=0, keepdims=True)
        v_ref[i:i + 1, :] = m
        vals = jnp.where(hit, NEG_INF, vals)


CAND_COUNTS = tuple(PEER_TOPK // (a + 1) for a in range(PEER_TOPK))
CAND_ROWS = -(-sum(CAND_COUNTS) // SUBLANES) * SUBLANES


def _peer_topk_kernel(hn_ref, wpq_ref, keys_ref, eid_ref, g_ref,
                      s1v_ref, s1i_ref, s2v_ref, s2i_ref, cand_ref, cid_ref, tv_ref, ti_ref):
    qv = _dot(hn_ref[...].astype(BF16), wpq_ref[...])
    K = PEER_TOPK
    n_cand = sum(CAND_COUNTS)
    tm = cand_ref.shape[1]
    cand_ref[n_cand:, :] = jnp.full((CAND_ROWS - n_cand, tm), NEG_INF, F32)
    cid_ref[n_cand:, :] = jnp.zeros((CAND_ROWS - n_cand, tm), F32)
    for h in range(PEER_HEADS):
        for x, (sv_ref, si_ref) in enumerate(((s1v_ref, s1i_ref), (s2v_ref, s2i_ref))):
            c0 = (h * 2 + x) * PEER_HALF
            q_hi, q_lo = _split_bf16(qv[:, c0:c0 + PEER_HALF])
            ql = jnp.concatenate([q_hi, q_lo], axis=1)
            st = lax.dot_general(keys_ref[h * 2 + x], ql, (((1,), (1,)), ((), ())),
                                 preferred_element_type=F32)
            _topk_rows(st, None, K, sv_ref, si_ref)
        s1, i1, s2, i2 = s1v_ref[...], s1i_ref[...], s2v_ref[...], s2i_ref[...]
        r0 = 0
        for a, cnt in enumerate(CAND_COUNTS):
            cand_ref[r0:r0 + cnt, :] = s1[a:a + 1, :] + s2[0:cnt, :]
            cid_ref[r0:r0 + cnt, :] = i1[a:a + 1, :] * float(PEER_KEYS) + i2[0:cnt, :]
            r0 += cnt
        _topk_rows(cand_ref[...], cid_ref[...], K, tv_ref, ti_ref)
        top = tv_ref[...]
        e = jnp.exp(top - top[0:1, :])
        g_ref[h * K:(h + 1) * K, :] = e / jnp.sum(e, axis=0, keepdims=True)
        eid_ref[h * K:(h + 1) * K, :] = ti_ref[...].astype(I32)


def _peer_topk(hn, wts, tm):
    n = hn.shape[0]
    assert n % tm == 0
    consts = wts["peer_topk"]
    rows = PEER_HEADS * PEER_TOPK
    col = pl.BlockSpec((rows, tm), lambda i: (0, i))
    K = PEER_TOPK
    return pl.pallas_call(
        _peer_topk_kernel,
        grid=(n // tm,),
        in_specs=[pl.BlockSpec((tm, D_MODEL), lambda i: (i, 0))] + [_vmem_full()] * len(consts),
        out_specs=(col, col),
        out_shape=(jax.ShapeDtypeStruct((rows, n), I32), jax.ShapeDtypeStruct((rows, n), F32)),
        scratch_shapes=[pltpu.VMEM((K, tm), F32)] * 4
                       + [pltpu.VMEM((CAND_ROWS, tm), F32)] * 2
                       + [pltpu.VMEM((K, tm), F32)] * 2,
        compiler_params=_cparams("parallel"),
        name="peer_topk",
    )(hn, *consts)


TABLE_ROWS = 4
HI_MASK = -65536


def _pack_table(tbl):
    e = tbl.shape[0]
    bits = lax.bitcast_convert_type(tbl.astype(BF16), jnp.uint16).astype(jnp.uint32)
    half = D_MODEL // 2
    word = bits[:, :half] | (bits[:, half:] << 16)
    return lax.bitcast_convert_type(word, I32).reshape(e, TABLE_ROWS, LANES)


def _unpack_row(x):
    lo = pltpu.bitcast(x << 16, F32)
    hi = pltpu.bitcast(x & HI_MASK, F32)
    return lo, hi


def _peer_u_kernel(eid_ref, h_ref, g_ref, u_ref, fold_ref, coef_ref, p_ref, act4_ref):
    tt = h_ref.shape[0]
    n_sel = eid_ref.shape[1]
    col_i = lax.broadcasted_iota(I32, (LANES, tt), 1)
    act4_ref[...] = jnp.zeros(act4_ref.shape, F32)
    p_ref[1] = jnp.zeros(p_ref.shape[1:], F32)

    def products(t, buf):
        hh = h_ref[t]
        h_lo, h_hi = hh[0:TABLE_ROWS], hh[TABLE_ROWS:]
        for j in range(n_sel):
            lo, hi = _unpack_row(u_ref[eid_ref[t, j]])
            p_ref[buf, j * TABLE_ROWS:(j + 1) * TABLE_ROWS, :] = lo * h_lo + hi * h_hi

    def lane_sums(t, buf):
        onehot = jnp.where(col_i == t, 1.0, 0.0).astype(BF16)
        act4_ref[...] += _dot(p_ref[buf].astype(BF16), onehot)

    def token_pair(i, carry):
        t0 = 2 * i
        products(t0, 0)
        lane_sums(t0 - 1, 1)
        products(t0 + 1, 1)
        lane_sums(t0, 0)
        return carry

    lax.fori_loop(0, tt // 2, token_pair, 0)
    lane_sums(tt - 1, 1)
    a_hi, a_lo = _split_bf16(act4_ref[...])
    act = _dot(fold_ref[...], a_hi) + _dot(fold_ref[...], a_lo)
    gelu = 0.5 * act * (1.0 + lax.erf(act * (1.0 / math.sqrt(2.0))))
    coef_ref[...] = g_ref[...] * gelu


N_ACC = 4


def _peer_v_kernel(eid_ref, coef_ref, x1_ref, v_ref, o_ref, crep_ref):
    tt = x1_ref.shape[0]
    n_sel = eid_ref.shape[1]
    lane = lax.broadcasted_iota(I32, (n_sel, tt), 1)

    def token(t, carry):
        c = jnp.sum(jnp.where(lane == t, coef_ref[...], 0.0), axis=-1, keepdims=True)
        crep_ref[...] = jnp.broadcast_to(c, crep_ref.shape)
        acc_lo = [jnp.zeros((TABLE_ROWS, LANES), F32) for _ in range(N_ACC)]
        acc_hi = [jnp.zeros((TABLE_ROWS, LANES), F32) for _ in range(N_ACC)]
        for j in range(n_sel):
            lo, hi = _unpack_row(v_ref[eid_ref[t, j]])
            cj = crep_ref[j:j + 1, :]
            acc_lo[j % N_ACC] = acc_lo[j % N_ACC] + cj * lo
            acc_hi[j % N_ACC] = acc_hi[j % N_ACC] + cj * hi
        ff = jnp.concatenate([sum(acc_lo[1:], acc_lo[0]), sum(acc_hi[1:], acc_hi[0])], axis=0)
        o_ref[t] = x1_ref[t] + ff
        return carry

    lax.fori_loop(0, tt, token, 0)


def _peer(x1, hn, wts, tm_topk, tt):
    n = x1.shape[0]
    assert n % tt == 0 and tt % 2 == 0
    n_sel = PEER_HEADS * PEER_TOPK
    eid_t, g_t = _peer_topk(hn, wts, tm_topk)
    eid = eid_t.T
    fold = jnp.asarray(np.repeat(np.eye(n_sel, dtype=np.float32), TABLE_ROWS, axis=1), BF16)
    tok3 = pl.BlockSpec((tt, SUBLANES, LANES), lambda i: (i, 0, 0))
    col = pl.BlockSpec((n_sel, tt), lambda i: (0, i))
    ids = pl.BlockSpec((tt, n_sel), lambda i: (i, 0), memory_space=pltpu.SMEM)
    coef = pl.pallas_call(
        _peer_u_kernel,
        grid=(n // tt,),
        in_specs=[ids, tok3, col, _vmem_full(), _vmem_full()],
        out_specs=col,
        out_shape=jax.ShapeDtypeStruct((n_sel, n), F32),
        scratch_shapes=[pltpu.VMEM((2, n_sel * TABLE_ROWS, LANES), F32),
                        pltpu.VMEM((n_sel * TABLE_ROWS, tt), F32)],
        compiler_params=_cparams("parallel"),
        name="peer_u",
    )(eid, hn.reshape(n, SUBLANES, LANES), g_t, wts["peer_u"], fold)
    out = pl.pallas_call(
        _peer_v_kernel,
        grid=(n // tt,),
        in_specs=[ids, col, tok3, _vmem_full()],
        out_specs=tok3,
        out_shape=jax.ShapeDtypeStruct((n, SUBLANES, LANES), F32),
        scratch_shapes=[pltpu.VMEM((n_sel, LANES), F32)],
        compiler_params=_cparams("parallel"),
        name="peer_v",
    )(eid, coef, x1.reshape(n, SUBLANES, LANES), wts["peer_v"])
    return out.reshape(n, D_MODEL)


IN_WIDTHS = (D_INNER, CONV_DIM, SSM_HEADS, D_MODEL, KV_DIM, KV_DIM, D_MODEL, N_BRANCHES * D_MODEL)


def _row(v):
    return v.reshape(1, -1).astype(F32)


def _prepare(p):
    offs = np.cumsum((0,) + IN_WIDTHS)
    seg = [p["w_in"][:, offs[i]:offs[i + 1]].astype(BF16) for i in range(len(IN_WIDTHS))]
    wz, wxbc, wdt, wq, wk, wv, wqm, wg = seg
    wdt = jnp.pad(wdt, ((0, 0), (0, LANES - SSM_HEADS)))
    e64, et64 = _group_indicators(D_MODEL, HEAD_DIM)
    e64k, et64k = _group_indicators(KV_DIM, HEAD_DIM)
    e256, et256 = _group_indicators(D_MODEL, MEM_HEAD_DIM)
    gq = _row(jnp.tile(p["g_q"], MOBA_HEADS)) * (HEAD_DIM ** -0.5 * LOG2E)
    gk = _row(jnp.tile(p["g_k"], MOBA_KV_HEADS))
    gmq = _row(jnp.tile(p["g_mq"], MEM_HEADS)) * MEM_HEAD_DIM ** -0.5
    inproj = [_row(p["g_mix_norm"]), wz, wxbc, wdt, wq, wk, wv, wqm, wg, _row(p["b_gate"]),
              gq, gk, gmq, e64, et64, e64k, et64k, e256, et256]

    pad_h = lambda v: jnp.pad(_row(v), ((0, 0), (0, LANES - SSM_HEADS)))
    e2 = np.zeros((LANES, D_INNER), np.float32)
    e2[np.arange(D_INNER) // SSM_HEAD_DIM, np.arange(D_INNER)] = 1.0
    e2 = jnp.asarray(np.concatenate([e2, e2], axis=0), BF16)
    eg, etg2 = _group_indicators(D_INNER, D_INNER // SSM_GROUPS)
    ssd = [p["conv_w"].astype(F32), _row(p["conv_b"]), pad_h(p["dt_bias"]), pad_h(p["a_log"]),
           _row(jnp.repeat(p["d_skip"], SSM_HEAD_DIM)), _row(p["g_ssm_norm"]), e2, eg, etg2]

    memkv = [_row(p["g_mem_norm"]), p["w_mem_kv"].astype(BF16),
             _row(jnp.tile(p["g_mk"], MEM_HEADS)), e256, et256]
    merge = [p["w_ssm_out"].astype(BF16), p["w_moba_out"].astype(BF16), p["w_mem_out"].astype(BF16),
             p["w_out"].astype(BF16), _row(p["g_ffn_norm"])]
    keys = p["peer_keys"].astype(BF16).reshape(PEER_HEADS * 2, PEER_KEYS, PEER_HALF)
    peer_topk = [p["w_pq"].astype(BF16), jnp.concatenate([keys, keys], axis=-1)]
    return {"inproj": inproj, "ssd": ssd, "memkv": memkv, "merge": merge, "peer_topk": peer_topk,
            "peer_u": _pack_table(p["peer_u"]), "peer_v": _pack_table(p["peer_v"])}


def _tile(n, pref):
    return pref if n % pref == 0 else n


def _group_forward(x, wts, ssd_state, attn_fn, mem_k, mem_v):
    b, t, _ = x.shape
    n = b * t
    x2 = x.reshape(n, D_MODEL)
    z, xbc, dt, q, k, v, qm, gate = _inproj(x2, wts, _tile(n, 256))
    conv0, ssm0, q_rows, n_valid = ssd_state
    y_ssd, ssm_new, conv_new = _ssd(xbc.reshape(b, t, CONV_DIM), z.reshape(b, t, D_INNER),
                                    dt.reshape(b, t, LANES), conv0, ssm0, wts, q_rows, n_valid)
    k3, v3 = k.reshape(b, t, KV_DIM), v.reshape(b, t, KV_DIM)
    y_moba = attn_fn(q.reshape(b, t, D_MODEL), k3, v3)
    y_mem = _memattn(qm.reshape(b, t, D_MODEL), mem_k, mem_v, _tile(t, 512))
    x1, hn = _merge(x2, y_ssd.reshape(n, D_INNER), y_moba.reshape(n, D_MODEL),
                    y_mem.reshape(n, D_MODEL), gate, wts, _tile(n, 256))
    out = _peer(x1, hn, wts, _tile(n, 256), _tile(n, 128))
    return (out.reshape(b, t, D_MODEL),
            ssm_new.reshape(b, SSM_HEADS, SSM_HEAD_DIM, D_STATE), conv_new,
            k3.reshape(b, t, MOBA_KV_HEADS, HEAD_DIM), v3.reshape(b, t, MOBA_KV_HEADS, HEAD_DIM))


def kernel(x_prompt, x_sample, cache_k, cache_v, state_ssm, state_conv, cache_mem_k, cache_mem_v,
           page_table, mem_prompt, g_mix_norm, w_in, conv_w, conv_b, dt_bias, a_log, d_skip,
           g_ssm_norm, w_ssm_out, g_q, g_k, rel_bias, w_moba_out, g_mem_norm, w_mem_kv, g_mq, g_mk,
           w_mem_out, b_gate, w_out, g_ffn_norm, w_pq, peer_keys, peer_u, peer_v):
    depth = w_in.shape[0]
    bp, tp, _ = x_prompt.shape
    sb, dec_seq, _ = x_sample.shape
    mem_len = mem_prompt.shape[1]
    n_pool, page = cache_k.shape[1], cache_k.shape[2]
    past_len = page_table.shape[1] * page
    assert past_len % MOBA_BLOCK == 0 and tp % MOBA_BLOCK == 0

    far_d = _far_distance()
    n_near_p = -(-(far_d + MOBA_BLOCK - 1) // MOBA_BLOCK) - 1
    bias_p = _bias_prompt(rel_bias, 2 * n_near_p + 2)
    farb_p = _row(jnp.repeat(rel_bias[NUM_BUCKETS - 1], Q_BLOCK)) * LOG2E
    nbp = past_len // MOBA_BLOCK
    n_far_s = min(max((past_len - (MOBA_BLOCK - 1) - far_d) // MOBA_BLOCK + 1, 0), nbp)
    bias_s = _bias_sample(rel_bias, past_len, n_far_s, nbp - n_far_s, dec_seq)
    farb_s = _row(jnp.repeat(rel_bias[NUM_BUCKETS - 1], dec_seq)) * LOG2E

    per_layer = dict(g_mix_norm=g_mix_norm, w_in=w_in, conv_w=conv_w, conv_b=conv_b, dt_bias=dt_bias,
                     a_log=a_log, d_skip=d_skip, g_ssm_norm=g_ssm_norm, w_ssm_out=w_ssm_out, g_q=g_q,
                     g_k=g_k, w_moba_out=w_moba_out, g_mem_norm=g_mem_norm, w_mem_kv=w_mem_kv,
                     g_mq=g_mq, g_mk=g_mk, w_mem_out=w_mem_out, b_gate=b_gate, w_out=w_out,
                     g_ffn_norm=g_ffn_norm, w_pq=w_pq, peer_keys=peer_keys, peer_u=peer_u,
                     peer_v=peer_v)
    xp, xs = x_prompt, x_sample
    outs = [[] for _ in range(10)]
    chunk = math.gcd(tp, SSD_CHUNK)
    for l in range(depth):
        wts = _prepare({name: val[l] for name, val in per_layer.items()})
        mk, mv = _memkv(mem_prompt.reshape(bp * mem_len, D_MODEL), wts)
        mk, mv = mk.reshape(bp, mem_len, D_MODEL), mv.reshape(bp, mem_len, D_MODEL)
        prompt_attn = functools.partial(_moba_prompt, bias_tiles=bias_p, farb=farb_p, n_near=n_near_p)
        xp, h_p, c_p, k_p, v_p = _group_forward(xp, wts, (None, None, chunk, chunk), prompt_attn, mk, mv)
        sample_attn = functools.partial(
            _moba_sample, cache_k=cache_k, cache_v=cache_v, layer=l, page_table=page_table,
            bias_tiles=bias_s, farb=farb_s, n_far=n_far_s, pages_per_step=8)
        conv0 = jnp.pad(state_conv[l], ((0, 0), (SUBLANES - (CONV_WIDTH - 1), 0), (0, 0)))
        ssm0 = state_ssm[l].reshape(sb, D_INNER, D_STATE)
        xs, h_s, c_s, k_s, v_s = _group_forward(
            xs, wts, (conv0, ssm0, Q_BLOCK, dec_seq), sample_attn,
            cache_mem_k[l].reshape(sb, mem_len, D_MODEL), cache_mem_v[l].reshape(sb, mem_len, D_MODEL))
        heads = (MEM_HEADS, MEM_HEAD_DIM)
        for lst, val in zip(outs, (k_p, v_p, k_s, v_s, h_p, h_s, c_p, c_s,
                                   mk.reshape(bp, mem_len, *heads), mv.reshape(bp, mem_len, *heads))):
            lst.append(val)
    return (xp, xs) + tuple(jnp.stack(lst) for lst in outs)
```

```python
import functools
import math

import numpy as np
import jax
import jax.numpy as jnp
from jax import lax
from jax.experimental import pallas as pl
from jax.experimental.pallas import tpu as pltpu

F32 = jnp.float32
BF16 = jnp.bfloat16
I32 = jnp.int32
NEG_INF = float("-inf")

D_MODEL = 1024
D_INNER = 2048
SSM_HEAD_DIM = 64
SSM_HEADS = 32
SSM_GROUPS = 4
D_STATE = 128
CONV_WIDTH = 4
CONV_DIM = D_INNER + 2 * SSM_GROUPS * D_STATE
SSD_CHUNK = 256
HEAD_DIM = 64
MOBA_HEADS = 16
MOBA_KV_HEADS = 4
Q_PER_KV = 4
KV_DIM = MOBA_KV_HEADS * HEAD_DIM
MOBA_BLOCK = 256
MOBA_TOPK = 3
Q_BLOCK = 128
NUM_BUCKETS = 32
MAX_DISTANCE = 2048
MEM_HEADS = 4
MEM_HEAD_DIM = 256
PEER_HEADS = 8
PEER_KEYS = 128
PEER_HALF = 128
PEER_TOPK = 16
N_BRANCHES = 3
EPS = 1e-6
LOG2E = math.log2(math.e)

LANES = 128
SUBLANES = 8
VMEM_LIMIT_BYTES = 56 * 1024 * 1024


def _cparams(*sem):
    return pltpu.CompilerParams(dimension_semantics=sem, vmem_limit_bytes=VMEM_LIMIT_BYTES)


def _vmem_full():
    return pl.BlockSpec(memory_space=pltpu.VMEM)


def _dot(a, b):
    return jnp.dot(a, b, preferred_element_type=F32)


def _split_bf16(x):
    hi = x.astype(BF16)
    lo = (x - hi.astype(F32)).astype(BF16)
    return hi, lo


def _group_indicators(dim, gsize):
    e = np.zeros((dim, LANES), np.float32)
    e[np.arange(dim), np.arange(dim) // gsize] = 1.0
    et2 = np.concatenate([e.T, e.T], axis=0)
    return jnp.asarray(e, BF16), jnp.asarray(et2, BF16)


def _group_rr(x, e_ref, et2_ref, gsize):
    ssq = _dot((x * x).astype(BF16), e_ref[...])
    r = lax.rsqrt(ssq * (1.0 / gsize) + EPS)
    r_hi, r_lo = _split_bf16(r)
    return _dot(jnp.concatenate([r_hi, r_lo], axis=1), et2_ref[...])


def _inproj_kernel(x_ref, gmix_ref, wz_ref, wxbc_ref, wdt_ref, wq_ref, wk_ref, wv_ref, wqm_ref,
                   wg_ref, bgate_ref, gq_ref, gk_ref, gmq_ref, e64_ref, et64_ref, e64k_ref,
                   et64k_ref, e256_ref, et256_ref,
                   z_ref, xbc_ref, dt_ref, q_ref, k_ref, v_ref, qm_ref, gate_ref):
    x = x_ref[...]
    h = (x * lax.rsqrt(jnp.mean(x * x, axis=-1, keepdims=True) + EPS) * gmix_ref[...]).astype(BF16)
    z_ref[...] = _dot(h, wz_ref[...]).astype(BF16)
    xbc_ref[...] = _dot(h, wxbc_ref[...])
    dt_ref[...] = _dot(h, wdt_ref[...])
    q = _dot(h, wq_ref[...])
    q_ref[...] = (q * _group_rr(q, e64_ref, et64_ref, HEAD_DIM) * gq_ref[...]).astype(BF16)
    k = _dot(h, wk_ref[...])
    k_ref[...] = k * _group_rr(k, e64k_ref, et64k_ref, HEAD_DIM) * gk_ref[...]
    v_ref[...] = _dot(h, wv_ref[...])
    qm = _dot(h, wqm_ref[...])
    qm_ref[...] = (qm * _group_rr(qm, e256_ref, et256_ref, MEM_HEAD_DIM) * gmq_ref[...]).astype(BF16)
    gate_ref[...] = jax.nn.sigmoid(_dot(h, wg_ref[...]) + bgate_ref[...]).astype(BF16)


def _inproj(x, wts, tm):
    n = x.shape[0]
    assert n % tm == 0
    consts = wts["inproj"]
    row = lambda w: pl.BlockSpec((tm, w), lambda i: (i, 0))
    out_shape = (
        jax.ShapeDtypeStruct((n, D_INNER), BF16),
        jax.ShapeDtypeStruct((n, CONV_DIM), F32),
        jax.ShapeDtypeStruct((n, LANES), F32),
        jax.ShapeDtypeStruct((n, D_MODEL), BF16),
        jax.ShapeDtypeStruct((n, KV_DIM), F32),
        jax.ShapeDtypeStruct((n, KV_DIM), F32),
        jax.ShapeDtypeStruct((n, D_MODEL), BF16),
        jax.ShapeDtypeStruct((n, N_BRANCHES * D_MODEL), BF16),
    )
    return pl.pallas_call(
        _inproj_kernel,
        grid=(n // tm,),
        in_specs=[row(D_MODEL)] + [_vmem_full()] * len(consts),
        out_specs=tuple(row(s.shape[1]) for s in out_shape),
        out_shape=out_shape,
        compiler_params=_cparams("parallel"),
        name="inproj",
    )(x, *consts)


def _ssd_kernel(*refs, q_rows, n_valid, has_init):
    if has_init:
        (xbc_ref, z_ref, dt_ref, conv0_ref, ssm0_ref, cw_ref, cb_ref, dtb_ref, alog_ref, dskip_ref,
         gssm_ref, e2_ref, eg_ref, etg2_ref, y_ref, ssm_out_ref, conv_out_ref,
         xw_ref, st_ref, yb_ref) = refs
    else:
        (xbc_ref, z_ref, dt_ref, cw_ref, cb_ref, dtb_ref, alog_ref, dskip_ref,
         gssm_ref, e2_ref, eg_ref, etg2_ref, y_ref, ssm_out_ref, conv_out_ref,
         xw_ref, st_ref, yb_ref) = refs
    c = pl.program_id(1)
    last = pl.num_programs(1) - 1
    Q, NV = q_rows, n_valid
    G = SSM_GROUPS
    GW = D_INNER // G

    @pl.when(c == 0)
    def _init():
        if has_init:
            xw_ref[0:SUBLANES, :] = conv0_ref[0]
            st_ref[...] = ssm0_ref[0].T
        else:
            xw_ref[0:SUBLANES, :] = jnp.zeros((SUBLANES, CONV_DIM), F32)
            st_ref[...] = jnp.zeros_like(st_ref)

    xw_ref[SUBLANES:SUBLANES + NV, :] = xbc_ref[0]
    if NV < Q:
        xw_ref[SUBLANES + NV:SUBLANES + Q, :] = jnp.zeros((Q - NV, CONV_DIM), F32)

    acc = jnp.broadcast_to(cb_ref[...], (Q, CONV_DIM))
    for kk in range(CONV_WIDTH):
        off = SUBLANES - (CONV_WIDTH - 1) + kk
        acc = acc + cw_ref[kk:kk + 1, :] * xw_ref[off:off + Q, :]
    xc = jax.nn.silu(acc)
    xs = xc[:, :D_INNER]
    bm = xc[:, D_INNER:D_INNER + G * D_STATE]
    cm = xc[:, D_INNER + G * D_STATE:]

    dtv = dt_ref[0]
    if NV < Q:
        dtv = jnp.concatenate([dtv, jnp.zeros((Q - NV, LANES), F32)], axis=0)
    xdt_pre = dtv + dtb_ref[...]
    dt = jnp.maximum(xdt_pre, 0.0) + jnp.log1p(jnp.exp(-jnp.abs(xdt_pre)))
    row_i = lax.broadcasted_iota(I32, (Q, LANES), 0)
    if NV < Q:
        dt = jnp.where(row_i < NV, dt, 0.0)
    a = -jnp.exp(alog_ref[...])
    da = dt * a

    ri = lax.broadcasted_iota(I32, (Q, Q), 0)
    ci = lax.broadcasted_iota(I32, (Q, Q), 1)
    causal = ri >= ci
    lower = causal.astype(F32)
    upper = (ri <= ci).astype(F32)
    cum = jnp.dot(lower, da, preferred_element_type=F32, precision=lax.Precision.HIGHEST)
    cum_t = jnp.dot(da.T, upper, preferred_element_type=F32, precision=lax.Precision.HIGHEST)

    def expand(v):
        hi, lo = _split_bf16(v)
        return _dot(jnp.concatenate([hi, lo], axis=1), e2_ref[...])

    dt_x = expand(dt)
    ecum_x = expand(jnp.exp(cum))
    toend_x = expand(jnp.exp(cum[Q - 1:Q, :] - cum))
    xdt = xs * dt_x
    xdt_b = xdt.astype(BF16)
    xw_b = (xdt * toend_x).astype(BF16)
    dec_row = ecum_x[Q - 1:Q, :]
    lane_lo = lax.broadcasted_iota(I32, (Q, LANES), 1) < SSM_HEAD_DIM

    for g in range(G):
        cg = cm[:, g * D_STATE:(g + 1) * D_STATE].astype(BF16)
        bg = bm[:, g * D_STATE:(g + 1) * D_STATE]
        bg_b = bg.astype(BF16)
        cb = lax.dot_general(cg, bg_b, (((1,), (1,)), ((), ())), preferred_element_type=F32)
        st_g = st_ref[:, g * GW:(g + 1) * GW]
        y_off = _dot(cg, st_g.astype(BF16))
        for pair in range(GW // LANES):
            col0 = g * GW + pair * LANES
            x2 = xdt_b[:, col0:col0 + LANES]
            halves = []
            for e in range(2):
                hh = col0 // SSM_HEAD_DIM + e
                seg = cum[:, hh:hh + 1] - cum_t[hh:hh + 1, :]
                lmat = jnp.exp(jnp.where(causal, seg, NEG_INF))
                halves.append(_dot((cb * lmat).astype(BF16), x2))
            yb_ref[:, col0:col0 + LANES] = jnp.where(lane_lo, halves[0], halves[1])
        yb_ref[:, g * GW:(g + 1) * GW] = (yb_ref[:, g * GW:(g + 1) * GW]
                                          + y_off * ecum_x[:, g * GW:(g + 1) * GW])
        st_ref[:, g * GW:(g + 1) * GW] = (st_g * dec_row[:, g * GW:(g + 1) * GW]
                                          + _dot(bg.T.astype(BF16), xw_b[:, g * GW:(g + 1) * GW]))

    y = yb_ref[...] + xs * dskip_ref[...]
    zf = z_ref[0].astype(F32)
    if NV < Q:
        zf = jnp.concatenate([zf, jnp.zeros((Q - NV, D_INNER), F32)], axis=0)
    yz = y * jax.nn.silu(zf)
    yn = yz * _group_rr(yz, eg_ref, etg2_ref, GW) * gssm_ref[...]
    y_ref[0] = yn[0:NV, :].astype(BF16)

    xw_ref[0:SUBLANES, :] = xw_ref[NV:NV + SUBLANES, :]

    @pl.when(c == last)
    def _fin():
        ssm_out_ref[0] = st_ref[...].T
        conv_out_ref[0] = xw_ref[SUBLANES - (CONV_WIDTH - 1):SUBLANES, :]


def _ssd(xbc, z, dt, conv0, ssm0, wts, q_rows, n_valid):
    b, t, _ = xbc.shape
    assert t % n_valid == 0 and n_valid >= CONV_WIDTH - 1
    nc = t // n_valid
    has_init = conv0 is not None
    consts = wts["ssd"]
    tile = lambda w: pl.BlockSpec((1, n_valid, w), lambda i, c: (i, c, 0))
    in_specs = [tile(CONV_DIM), tile(D_INNER), tile(LANES)]
    args = [xbc, z, dt]
    if has_init:
        in_specs += [pl.BlockSpec((1, SUBLANES, CONV_DIM), lambda i, c: (i, 0, 0)),
                     pl.BlockSpec((1, D_INNER, D_STATE), lambda i, c: (i, 0, 0))]
        args += [conv0, ssm0]
    in_specs += [_vmem_full()] * len(consts)
    out_shape = (jax.ShapeDtypeStruct((b, t, D_INNER), BF16),
                 jax.ShapeDtypeStruct((b, D_INNER, D_STATE), F32),
                 jax.ShapeDtypeStruct((b, CONV_WIDTH - 1, CONV_DIM), F32))
    out_specs = (tile(D_INNER),
                 pl.BlockSpec((1, D_INNER, D_STATE), lambda i, c: (i, 0, 0)),
                 pl.BlockSpec((1, CONV_WIDTH - 1, CONV_DIM), lambda i, c: (i, 0, 0)))
    return pl.pallas_call(
        functools.partial(_ssd_kernel, q_rows=q_rows, n_valid=n_valid, has_init=has_init),
        grid=(b, nc),
        in_specs=in_specs,
        out_specs=out_specs,
        out_shape=out_shape,
        scratch_shapes=[pltpu.VMEM((SUBLANES + q_rows, CONV_DIM), F32),
                        pltpu.VMEM((D_STATE, D_INNER), F32),
                        pltpu.VMEM((q_rows, D_INNER), F32)],
        compiler_params=_cparams("parallel", "arbitrary"),
        name="ssd_scan",
    )(*args, *consts)


def _rel_bucket(dist):
    n = jnp.maximum(dist, 0)
    max_exact = NUM_BUCKETS // 2
    nf = jnp.maximum(n, 1).astype(F32)
    large = max_exact + (jnp.log(nf / max_exact) / math.log(MAX_DISTANCE / max_exact)
                         * (NUM_BUCKETS - max_exact)).astype(I32)
    large = jnp.minimum(large, NUM_BUCKETS - 1)
    return jnp.where(n < max_exact, n, large)


def _far_distance():
    d = np.arange(1, 4 * MAX_DISTANCE, dtype=np.float64)
    large = 16 + np.floor(np.log(d / 16) / math.log(MAX_DISTANCE / 16) * 16)
    below = np.nonzero(large < NUM_BUCKETS - 1)[0]
    return int(d[below[-1]]) + 1 + 1


def _bias_from_dist(dist, rbl_ref):
    bucket = _rel_bucket(dist)
    val = jnp.zeros(dist.shape, F32)
    for b in range(NUM_BUCKETS):
        val = jnp.where(bucket == b, rbl_ref[b:b + 1, :], val)
    return jnp.where(dist >= 0, val * LOG2E, NEG_INF)


def _bias_prompt_kernel(rbl_ref, o_ref):
    di = pl.program_id(0)
    w = Q_PER_KV * Q_BLOCK
    j = lax.broadcasted_iota(I32, (MOBA_BLOCK, w), 0)
    i = lax.broadcasted_iota(I32, (MOBA_BLOCK, w), 1) % Q_BLOCK
    o_ref[0, 0] = _bias_from_dist(di * Q_BLOCK + i - j, rbl_ref.at[0]).astype(BF16)


def _bias_prompt(rel_bias, n_tiles):
    w = Q_PER_KV * Q_BLOCK
    rbl = jnp.repeat(rel_bias.reshape(NUM_BUCKETS, MOBA_KV_HEADS, Q_PER_KV).transpose(1, 0, 2),
                     Q_BLOCK, axis=2)
    return pl.pallas_call(
        _bias_prompt_kernel,
        grid=(n_tiles, MOBA_KV_HEADS),
        in_specs=[pl.BlockSpec((1, NUM_BUCKETS, w), lambda d, h: (h, 0, 0))],
        out_specs=pl.BlockSpec((1, 1, MOBA_BLOCK, w), lambda d, h: (d, h, 0, 0)),
        out_shape=jax.ShapeDtypeStruct((n_tiles, MOBA_KV_HEADS, MOBA_BLOCK, w), BF16),
        compiler_params=_cparams("parallel", "parallel"),
        name="moba_bias_prompt",
    )(rbl)


def _bias_sample_kernel(rbl_ref, o_ref, *, past_len, first_block, n_near, dec_seq):
    r = pl.program_id(0)
    kpos0 = jnp.where(r < n_near, (first_block + r) * MOBA_BLOCK, past_len)
    j = lax.broadcasted_iota(I32, (MOBA_BLOCK, LANES), 0)
    t = lax.broadcasted_iota(I32, (MOBA_BLOCK, LANES), 1) % dec_seq
    o_ref[0] = _bias_from_dist(past_len + t - (kpos0 + j), rbl_ref)


def _bias_sample(rel_bias, past_len, first_block, n_near, dec_seq):
    rbl = jnp.repeat(rel_bias, dec_seq, axis=1)
    return pl.pallas_call(
        functools.partial(_bias_sample_kernel, past_len=past_len, first_block=first_block,
                          n_near=n_near, dec_seq=dec_seq),
        grid=(n_near + 1,),
        in_specs=[_vmem_full()],
        out_specs=pl.BlockSpec((1, MOBA_BLOCK, LANES), lambda r: (r, 0, 0)),
        out_shape=jax.ShapeDtypeStruct((n_near + 1, MOBA_BLOCK, LANES), F32),
        compiler_params=_cparams("parallel"),
        name="moba_bias_sample",
    )(rbl)


def _kmean_kernel(k_ref, o_ref, *, nb):
    for b in range(nb):
        blk = k_ref[0, b * MOBA_BLOCK:(b + 1) * MOBA_BLOCK, :]
        o_ref[0, b:b + 1, :] = jnp.sum(blk, axis=0, keepdims=True) * (1.0 / MOBA_BLOCK)


def _kmean(k):
    b, t, _ = k.shape
    nb = t // MOBA_BLOCK
    return pl.pallas_call(
        functools.partial(_kmean_kernel, nb=nb),
        grid=(b,),
        in_specs=[pl.BlockSpec((1, t, KV_DIM), lambda i: (i, 0, 0))],
        out_specs=pl.BlockSpec((1, nb, KV_DIM), lambda i: (i, 0, 0)),
        out_shape=jax.ShapeDtypeStruct((b, nb, KV_DIM), F32),
        compiler_params=_cparams("parallel"),
        name="moba_kmean",
    )(k)


def _select_blocks(gate, own, n_sel_rows):
    nb = gate.shape[0]
    blk = lax.broadcasted_iota(I32, gate.shape, 0)
    sel = jnp.zeros(gate.shape, F32)
    for t in range(MOBA_TOPK):
        m = jnp.max(gate, axis=0, keepdims=True)
        pos = jnp.min(jnp.where(gate == m, blk, nb), axis=0, keepdims=True)
        hit = blk == pos
        sel = jnp.where(hit, jnp.maximum(sel, jnp.where(t < n_sel_rows, 1.0, 0.0)), sel)
        gate = jnp.where(hit, NEG_INF, gate)
    return sel, blk


V_ROWS = HEAD_DIM + 16


def _moba_prompt_kernel(qt_ref, k_ref, vt_ref, km_ref, bias_ref, farb_ref, o_ref,
                        qx_ref, add_ref, m_ref, acc_ref, s_ref, *, n_near):
    a = pl.program_id(1)
    own = a // 2
    par = a % 2
    nb = k_ref.shape[1]
    hw = Q_PER_KV * Q_BLOCK
    w = MOBA_KV_HEADS * hw

    row_h = lax.broadcasted_iota(I32, (KV_DIM, Q_BLOCK), 0) // HEAD_DIM
    for h in range(MOBA_KV_HEADS):
        for g in range(Q_PER_KV):
            piece = jnp.where(row_h == h, qt_ref[0, g], jnp.zeros((), BF16))
            col = (h * Q_PER_KV + g) * Q_BLOCK
            qx_ref[:, col:col + Q_BLOCK] = piece

    km_hi, km_lo = _split_bf16(km_ref[0])
    gate = _dot(km_hi, qx_ref[...]) + _dot(km_lo, qx_ref[...])
    blk0 = lax.broadcasted_iota(I32, (nb, w), 0)
    gate = jnp.where(blk0 < own, gate, NEG_INF)
    sel, blk = _select_blocks(gate, own, own)
    far = (own - blk) > n_near
    add = jnp.where(sel > 0.0, jnp.where(far, farb_ref[...], 0.0), NEG_INF)
    add_ref[...] = jnp.where(blk == own, 0.0, add)

    m_ref[...] = jnp.full(m_ref.shape, NEG_INF, F32)
    acc_ref[...] = jnp.zeros(acc_ref.shape, F32)

    def scores(b, slot):
        kb = k_ref[0, b]
        for h in range(MOBA_KV_HEADS):
            s_ref[slot, h] = _dot(kb, qx_ref[:, h * hw:(h + 1) * hw])

    def attend(b, slot, di):
        for h in range(MOBA_KV_HEADS):
            s = s_ref[slot, h]
            if di is not None:
                s = s + bias_ref[di, h].astype(F32)
            add = add_ref[pl.ds(b, 1), h * hw:(h + 1) * hw]
            m_prev = m_ref[h]
            m_new = jnp.maximum(m_prev, jnp.max(s, axis=0, keepdims=True) + add)
            p = jnp.exp2(s - (m_new - add))
            alpha = jnp.exp2(m_prev - m_new)
            vth = vt_ref[0, b, h * V_ROWS:(h + 1) * V_ROWS, :]
            acc_ref[h] = alpha * acc_ref[h] + _dot(vth, p.astype(BF16))
            m_ref[h] = m_new

    n_biased = jnp.minimum(n_near, own) + 1

    def near_body(idx, carry):
        scores(own - idx, 0)
        attend(own - idx, 0, par + 2 * idx)
        return carry

    lax.fori_loop(0, n_biased, near_body, 0)

    def far_body(idx, carry):
        scores(own - idx, 0)
        attend(own - idx, 0, None)
        return carry

    lax.fori_loop(n_biased, own + 1, far_body, 0)

    for h in range(MOBA_KV_HEADS):
        acc = acc_ref[h]
        o_ref[0, h] = (acc[0:HEAD_DIM, :] / acc[HEAD_DIM:HEAD_DIM + 1, :]).astype(BF16)


def _moba_prompt(q, k, v, bias_tiles, farb, n_near):
    b, t, _ = q.shape
    assert t % MOBA_BLOCK == 0
    nb, nq = t // MOBA_BLOCK, t // Q_BLOCK
    hw = Q_PER_KV * Q_BLOCK
    w = MOBA_KV_HEADS * hw
    km = _kmean(k)
    qt = q.reshape(b, t, MOBA_KV_HEADS, Q_PER_KV, HEAD_DIM).transpose(0, 3, 2, 4, 1).reshape(
        b, Q_PER_KV, KV_DIM, t)
    kb = k.astype(BF16).reshape(b, nb, MOBA_BLOCK, KV_DIM)
    vt = v.astype(BF16).reshape(b, nb, MOBA_BLOCK, MOBA_KV_HEADS, HEAD_DIM).transpose(0, 1, 3, 4, 2)
    ones_rows = jnp.zeros((V_ROWS - HEAD_DIM, MOBA_BLOCK), BF16).at[0].set(1.0)
    vt = jnp.concatenate(
        [vt, jnp.broadcast_to(ones_rows, (b, nb, MOBA_KV_HEADS) + ones_rows.shape)], axis=3)
    vt = vt.reshape(b, nb, MOBA_KV_HEADS * V_ROWS, MOBA_BLOCK)
    out = pl.pallas_call(
        functools.partial(_moba_prompt_kernel, n_near=n_near),
        grid=(b, nq),
        in_specs=[pl.BlockSpec((1, Q_PER_KV, KV_DIM, Q_BLOCK), lambda i, a: (i, 0, 0, a)),
                  pl.BlockSpec((1, nb, MOBA_BLOCK, KV_DIM), lambda i, a: (i, 0, 0, 0)),
                  pl.BlockSpec((1, nb, MOBA_KV_HEADS * V_ROWS, MOBA_BLOCK), lambda i, a: (i, 0, 0, 0)),
                  pl.BlockSpec((1, nb, KV_DIM), lambda i, a: (i, 0, 0)),
                  _vmem_full(), _vmem_full()],
        out_specs=pl.BlockSpec((1, MOBA_KV_HEADS, HEAD_DIM, hw), lambda i, a: (i, 0, 0, a)),
        out_shape=jax.ShapeDtypeStruct((b, MOBA_KV_HEADS, HEAD_DIM, nq * hw), BF16),
        scratch_shapes=[pltpu.VMEM((KV_DIM, w), BF16),
                        pltpu.VMEM((nb, w), F32),
                        pltpu.VMEM((MOBA_KV_HEADS, 1, hw), F32),
                        pltpu.VMEM((MOBA_KV_HEADS, V_ROWS, hw), F32),
                        pltpu.VMEM((1, MOBA_KV_HEADS, MOBA_BLOCK, hw), F32)],
        compiler_params=_cparams("parallel", "arbitrary"),
        name="moba_prompt",
    )(qt, kb, vt, km, bias_tiles, farb)
    out = out.reshape(b, MOBA_KV_HEADS, HEAD_DIM, nq, Q_PER_KV, Q_BLOCK).transpose(0, 3, 5, 1, 4, 2)
    return out.reshape(b, t, D_MODEL)


def _moba_sample_kernel(pt_ref, qx_ref, *rest, pages_per_step, n_far, dec_seq):
    del pt_ref
    gp = pages_per_step
    k_refs, v_refs = rest[:gp], rest[gp:2 * gp]
    (knew_ref, vnew_ref, biass_ref, farb_ref, o_ref,
     st_ref, vt_ref, gs_ref, add_ref) = rest[2 * gp:]
    tn = (((0,), (0,)), ((), ()))
    j = pl.program_id(1)
    nbp = st_ref.shape[0]
    n_near = nbp - n_far
    page = MOBA_BLOCK // 2
    qx = qx_ref[0]

    for i in range(gp):
        blk = j * (gp // 2) + i // 2
        half = i % 2
        keys = slice(half * page, (half + 1) * page)
        s = lax.dot_general(k_refs[i][0].astype(BF16), qx, tn, preferred_element_type=F32)
        st_ref[blk, keys, :] = s
        vt_ref[blk, :, keys] = v_refs[i][0].astype(BF16)
        ssum = jnp.sum(s, axis=0, keepdims=True)
        if half == 0:
            gs_ref[pl.ds(blk, 1), :] = ssum
        else:
            gs_ref[pl.ds(blk, 1), :] = gs_ref[pl.ds(blk, 1), :] + ssum

    @pl.when(j == pl.num_programs(1) - 1)
    def _fin():
        gate = gs_ref[...] * (1.0 / MOBA_BLOCK)
        sel, blk = _select_blocks(gate, nbp, nbp)
        add_ref[...] = jnp.where(sel > 0.0, jnp.where(blk < n_far, farb_ref[...], 0.0), NEG_INF)

        s_own = _dot(knew_ref[0].astype(BF16), qx) + biass_ref[n_near, 0:Q_BLOCK, :]
        m = jnp.max(s_own, axis=0, keepdims=True)

        def far_logits(b, m):
            lg = st_ref[b] + add_ref[pl.ds(b, 1), :]
            st_ref[b] = lg
            return jnp.maximum(m, jnp.max(lg, axis=0, keepdims=True))

        m = lax.fori_loop(0, n_far, far_logits, m)
        for r in range(n_near):
            b = n_far + r
            lg = st_ref[b] + add_ref[b:b + 1, :] + biass_ref[r]
            st_ref[b] = lg
            m = jnp.maximum(m, jnp.max(lg, axis=0, keepdims=True))

        p_own = jnp.exp2(s_own - m)
        l0 = jnp.sum(p_own, axis=0, keepdims=True)
        acc0 = _dot(vnew_ref[0].T.astype(BF16), p_own.astype(BF16))

        def weighted(b, carry):
            l, acc = carry
            p = jnp.exp2(st_ref[b] - m)
            return l + jnp.sum(p, axis=0, keepdims=True), acc + _dot(vt_ref[b], p.astype(BF16))

        l, acc = lax.fori_loop(0, nbp, weighted, (l0, acc0))
        y = acc / l
        lane_h = lax.broadcasted_iota(I32, (HEAD_DIM, LANES), 1) // (Q_PER_KV * dec_seq)
        out = jnp.zeros((HEAD_DIM, LANES), F32)
        for h in range(MOBA_KV_HEADS):
            out = jnp.where(lane_h == h, y[h * HEAD_DIM:(h + 1) * HEAD_DIM, :], out)
        o_ref[0] = out


def _moba_sample(q, k_new, v_new, cache_k, cache_v, page_table, bias_tiles, farb, n_far,
                 pages_per_step):
    s, dec_seq, _ = q.shape
    n_pool, page = cache_k.shape[:2]
    n_pages = page_table.shape[1]
    assert page * 2 == MOBA_BLOCK and n_pages % pages_per_step == 0 and pages_per_step % 2 == 0
    assert MOBA_HEADS * dec_seq == LANES and dec_seq <= Q_BLOCK
    nbp = n_pages // 2
    gp = pages_per_step
    q5 = q.reshape(s, dec_seq, MOBA_KV_HEADS, Q_PER_KV, HEAD_DIM)
    qx = jnp.einsum("sthgd,ph->spdhgt", q5, jnp.eye(MOBA_KV_HEADS, dtype=q.dtype)).reshape(
        s, KV_DIM, LANES)
    pad = ((0, 0), (0, Q_BLOCK - dec_seq), (0, 0))
    k_pad, v_pad = jnp.pad(k_new, pad), jnp.pad(v_new, pad)
    kt = cache_k.transpose(0, 2, 3, 1).reshape(n_pool, KV_DIM, page)
    vt = cache_v.transpose(0, 2, 3, 1).reshape(n_pool, KV_DIM, page)

    def page_spec(i):
        return pl.BlockSpec((1, KV_DIM, page), lambda b, j, pt: (pt[b, j * gp + i], 0, 0))

    seq_spec = lambda r, w: pl.BlockSpec((1, r, w), lambda b, j, pt: (b, 0, 0))
    grid_spec = pltpu.PrefetchScalarGridSpec(
        num_scalar_prefetch=1,
        grid=(s, n_pages // gp),
        in_specs=([seq_spec(KV_DIM, LANES)] + [page_spec(i) for i in range(gp)] * 2
                  + [seq_spec(Q_BLOCK, KV_DIM), seq_spec(Q_BLOCK, KV_DIM), _vmem_full(), _vmem_full()]),
        out_specs=seq_spec(HEAD_DIM, LANES),
        scratch_shapes=[pltpu.VMEM((nbp, MOBA_BLOCK, LANES), F32),
                        pltpu.VMEM((nbp, KV_DIM, MOBA_BLOCK), BF16),
                        pltpu.VMEM((nbp, LANES), F32),
                        pltpu.VMEM((nbp, LANES), F32)],
    )
    out = pl.pallas_call(
        functools.partial(_moba_sample_kernel, pages_per_step=gp, n_far=n_far, dec_seq=dec_seq),
        grid_spec=grid_spec,
        out_shape=jax.ShapeDtypeStruct((s, HEAD_DIM, LANES), F32),
        compiler_params=_cparams("parallel", "arbitrary"),
        name="moba_sample",
    )(page_table, qx, *([kt] * gp), *([vt] * gp), k_pad, v_pad, bias_tiles, farb)
    out = out.reshape(s, HEAD_DIM, MOBA_KV_HEADS, Q_PER_KV, dec_seq).transpose(0, 4, 2, 3, 1)
    return out.reshape(s, dec_seq, D_MODEL).astype(BF16)


def _memkv_kernel(mem_ref, gn_ref, w_ref, gk_ref, e_ref, et_ref, k_ref, v_ref):
    x = mem_ref[...]
    h = (x * lax.rsqrt(jnp.mean(x * x, axis=-1, keepdims=True) + EPS) * gn_ref[...]).astype(BF16)
    kv = _dot(h, w_ref[...])
    k = kv[:, :D_MODEL]
    k_ref[...] = k * _group_rr(k, e_ref, et_ref, MEM_HEAD_DIM) * gk_ref[...]
    v_ref[...] = kv[:, D_MODEL:]


def _memkv(mem, wts):
    n = mem.shape[0]
    tm = min(n, 256)
    assert n % tm == 0
    consts = wts["memkv"]
    row = pl.BlockSpec((tm, D_MODEL), lambda i: (i, 0))
    return pl.pallas_call(
        _memkv_kernel,
        grid=(n // tm,),
        in_specs=[row] + [_vmem_full()] * len(consts),
        out_specs=(row, row),
        out_shape=(jax.ShapeDtypeStruct((n, D_MODEL), F32),) * 2,
        compiler_params=_cparams("parallel"),
        name="mem_kv",
    )(mem, *consts)


def _memattn_kernel(q_ref, k_ref, v_ref, o_ref):
    q = q_ref[0]
    for h in range(MEM_HEADS):
        sl = slice(h * MEM_HEAD_DIM, (h + 1) * MEM_HEAD_DIM)
        kh = k_ref[0, :, sl].astype(BF16)
        vh = v_ref[0, :, sl].astype(BF16)
        s = lax.dot_general(q[:, sl], kh, (((1,), (1,)), ((), ())), preferred_element_type=F32)
        p = jnp.exp(s - jnp.max(s, axis=-1, keepdims=True))
        l = jnp.sum(p, axis=-1, keepdims=True)
        o_ref[0, :, sl] = (_dot(p.astype(BF16), vh) / l).astype(BF16)


def _memattn(qm, mk, mv, tq):
    b, t, _ = qm.shape
    m = mk.shape[1]
    assert t % tq == 0
    qspec = pl.BlockSpec((1, tq, D_MODEL), lambda i, a: (i, a, 0))
    mspec = pl.BlockSpec((1, m, D_MODEL), lambda i, a: (i, 0, 0))
    return pl.pallas_call(
        _memattn_kernel,
        grid=(b, t // tq),
        in_specs=[qspec, mspec, mspec],
        out_specs=qspec,
        out_shape=jax.ShapeDtypeStruct((b, t, D_MODEL), BF16),
        compiler_params=_cparams("parallel", "arbitrary"),
        name="mem_attn",
    )(qm, mk, mv)


def _merge_kernel(x_ref, ys_ref, ym_ref, yc_ref, gate_ref, ws_ref, wm_ref, wc_ref, wo_ref, gf_ref,
                  x1_ref, hn_ref):
    g = gate_ref[...].astype(F32)
    merged = (g[:, :D_MODEL] * _dot(ys_ref[...], ws_ref[...])
              + g[:, D_MODEL:2 * D_MODEL] * _dot(ym_ref[...], wm_ref[...])
              + g[:, 2 * D_MODEL:] * _dot(yc_ref[...], wc_ref[...]))
    x1 = x_ref[...] + _dot(merged.astype(BF16), wo_ref[...])
    x1_ref[...] = x1
    hn_ref[...] = x1 * lax.rsqrt(jnp.mean(x1 * x1, axis=-1, keepdims=True) + EPS) * gf_ref[...]


def _merge(x, y_ssd, y_moba, y_mem, gate, wts, tm):
    n = x.shape[0]
    assert n % tm == 0
    consts = wts["merge"]
    row = lambda w: pl.BlockSpec((tm, w), lambda i: (i, 0))
    return pl.pallas_call(
        _merge_kernel,
        grid=(n // tm,),
        in_specs=[row(D_MODEL), row(D_INNER), row(D_MODEL), row(D_MODEL), row(N_BRANCHES * D_MODEL)]
                 + [_vmem_full()] * len(consts),
        out_specs=(row(D_MODEL), row(D_MODEL)),
        out_shape=(jax.ShapeDtypeStruct((n, D_MODEL), F32),) * 2,
        compiler_params=_cparams("parallel"),
        name="merge_out",
    )(x, y_ssd, y_moba, y_mem, gate, *consts)


def _topk_rows(vals, payload, k, v_ref, p_ref):
    r = vals.shape[0]
    rows = lax.broadcasted_iota(I32, vals.shape, 0).astype(F32)
    for i in range(k):
        m = jnp.max(vals, axis=0, keepdims=True)
        pos = jnp.min(jnp.where(vals == m, rows, float(r)), axis=0, keepdims=True)
        hit = rows == pos
        if payload is None:
            p_ref[i:i + 1, :] = pos
        else:
            p_ref[i:i + 1, :] = jnp.max(jnp.where(hit, payload, -1.0), axis=0, keepdims=True)
        v_ref[i:i + 1, :] = m
        vals = jnp.where(hit, NEG_INF, vals)


CAND_COUNTS = tuple(PEER_TOPK // (a + 1) for a in range(PEER_TOPK))
CAND_ROWS = -(-sum(CAND_COUNTS) // SUBLANES) * SUBLANES


def _peer_topk_kernel(hn_ref, wpq_ref, keys_ref, eid_ref, g_ref,
                      s1v_ref, s1i_ref, s2v_ref, s2i_ref, cand_ref, cid_ref, tv_ref, ti_ref):
    qv = _dot(hn_ref[...].astype(BF16), wpq_ref[...])
    K = PEER_TOPK
    n_cand = sum(CAND_COUNTS)
    tm = cand_ref.shape[1]
    cand_ref[n_cand:, :] = jnp.full((CAND_ROWS - n_cand, tm), NEG_INF, F32)
    cid_ref[n_cand:, :] = jnp.zeros((CAND_ROWS - n_cand, tm), F32)
    for h in range(PEER_HEADS):
        for x, (sv_ref, si_ref) in enumerate(((s1v_ref, s1i_ref), (s2v_ref, s2i_ref))):
            c0 = (h * 2 + x) * PEER_HALF
            q_hi, q_lo = _split_bf16(qv[:, c0:c0 + PEER_HALF])
            ql = jnp.concatenate([q_hi, q_lo], axis=1)
            st = lax.dot_general(keys_ref[h * 2 + x], ql, (((1,), (1,)), ((), ())),
                                 preferred_element_type=F32)
            _topk_rows(st, None, K, sv_ref, si_ref)
        s1, i1, s2, i2 = s1v_ref[...], s1i_ref[...], s2v_ref[...], s2i_ref[...]
        r0 = 0
        for a, cnt in enumerate(CAND_COUNTS):
            cand_ref[r0:r0 + cnt, :] = s1[a:a + 1, :] + s2[0:cnt, :]
            cid_ref[r0:r0 + cnt, :] = i1[a:a + 1, :] * float(PEER_KEYS) + i2[0:cnt, :]
            r0 += cnt
        _topk_rows(cand_ref[...], cid_ref[...], K, tv_ref, ti_ref)
        top = tv_ref[...]
        e = jnp.exp(top - top[0:1, :])
        g_ref[h * K:(h + 1) * K, :] = e / jnp.sum(e, axis=0, keepdims=True)
        eid_ref[h * K:(h + 1) * K, :] = ti_ref[...].astype(I32)


def _peer_topk(hn, wts, tm):
    n = hn.shape[0]
    assert n % tm == 0
    consts = wts["peer_topk"]
    rows = PEER_HEADS * PEER_TOPK
    col = pl.BlockSpec((rows, tm), lambda i: (0, i))
    K = PEER_TOPK
    return pl.pallas_call(
        _peer_topk_kernel,
        grid=(n // tm,),
        in_specs=[pl.BlockSpec((tm, D_MODEL), lambda i: (i, 0))] + [_vmem_full()] * len(consts),
        out_specs=(col, col),
        out_shape=(jax.ShapeDtypeStruct((rows, n), I32), jax.ShapeDtypeStruct((rows, n), F32)),
        scratch_shapes=[pltpu.VMEM((K, tm), F32)] * 4
                       + [pltpu.VMEM((CAND_ROWS, tm), F32)] * 2
                       + [pltpu.VMEM((K, tm), F32)] * 2,
        compiler_params=_cparams("parallel"),
        name="peer_topk",
    )(hn, *consts)


TABLE_ROWS = 4
HI_MASK = -65536


def _pack_table(tbl):
    e = tbl.shape[0]
    bits = lax.bitcast_convert_type(tbl.astype(BF16), jnp.uint16).astype(jnp.uint32)
    half = D_MODEL // 2
    word = bits[:, :half] | (bits[:, half:] << 16)
    return lax.bitcast_convert_type(word, I32).reshape(e, TABLE_ROWS, LANES)


def _unpack_row(x):
    lo = pltpu.bitcast(x << 16, F32)
    hi = pltpu.bitcast(x & HI_MASK, F32)
    return lo, hi


def _peer_u_kernel(eid_ref, h_ref, g_ref, u_ref, fold_ref, coef_ref, p_ref, act4_ref):
    tt = h_ref.shape[0]
    n_sel = eid_ref.shape[1]
    grp = p_ref.shape[1] // LANES
    row_tok = lax.broadcasted_iota(I32, (grp * LANES, tt), 0) // LANES
    col_i = lax.broadcasted_iota(I32, (grp * LANES, tt), 1)
    act4_ref[...] = jnp.zeros(act4_ref.shape, F32)

    def token_group(gi, carry):
        t0 = gi * grp
        for g in range(grp):
            hh = h_ref[t0 + g]
            h_lo, h_hi = hh[0:TABLE_ROWS], hh[TABLE_ROWS:]
            for j in range(n_sel):
                lo, hi = _unpack_row(u_ref[eid_ref[t0 + g, j]])
                p_ref[j * TABLE_ROWS:(j + 1) * TABLE_ROWS, g * LANES:(g + 1) * LANES] = (
                    lo * h_lo + hi * h_hi)
        onehot = jnp.where(col_i == t0 + row_tok, 1.0, 0.0).astype(BF16)
        act4_ref[...] += _dot(p_ref[...].astype(BF16), onehot)
        return carry

    lax.fori_loop(0, tt // grp, token_group, 0)
    a_hi, a_lo = _split_bf16(act4_ref[...])
    act = _dot(fold_ref[...], a_hi) + _dot(fold_ref[...], a_lo)
    gelu = 0.5 * act * (1.0 + lax.erf(act * (1.0 / math.sqrt(2.0))))
    coef_ref[...] = g_ref[...] * gelu


N_ACC = 4


def _peer_v_kernel(eid_ref, coef_ref, x1_ref, v_ref, o_ref, crep_ref):
    tt = x1_ref.shape[0]
    n_sel = eid_ref.shape[1]
    lane = lax.broadcasted_iota(I32, (n_sel, tt), 1)

    def spread(t, buf):
        c = jnp.sum(jnp.where(lane == t, coef_ref[...], 0.0), axis=-1, keepdims=True)
        crep_ref[buf] = jnp.broadcast_to(c, (n_sel, LANES))

    def weighted_rows(t, buf):
        acc_lo = [jnp.zeros((TABLE_ROWS, LANES), F32) for _ in range(N_ACC)]
        acc_hi = [jnp.zeros((TABLE_ROWS, LANES), F32) for _ in range(N_ACC)]
        for j in range(n_sel):
            lo, hi = _unpack_row(v_ref[eid_ref[t, j]])
            cj = crep_ref[buf, j:j + 1, :]
            acc_lo[j % N_ACC] = acc_lo[j % N_ACC] + cj * lo
            acc_hi[j % N_ACC] = acc_hi[j % N_ACC] + cj * hi
        ff = jnp.concatenate([sum(acc_lo[1:], acc_lo[0]), sum(acc_hi[1:], acc_hi[0])], axis=0)
        o_ref[t] = x1_ref[t] + ff

    def token_pair(i, carry):
        t0 = 2 * i
        spread(t0 + 1, 1)
        weighted_rows(t0, 0)
        spread(jnp.minimum(t0 + 2, tt - 1), 0)
        weighted_rows(t0 + 1, 1)
        return carry

    spread(0, 0)
    lax.fori_loop(0, tt // 2, token_pair, 0)


def _peer(x1, hn, wts, tm_topk, tt):
    n = x1.shape[0]
    grp = 8 if tt % 8 == 0 else 1
    assert n % tt == 0 and tt % 2 == 0
    n_sel = PEER_HEADS * PEER_TOPK
    eid_t, g_t = _peer_topk(hn, wts, tm_topk)
    eid = eid_t.T
    fold = jnp.asarray(np.repeat(np.eye(n_sel, dtype=np.float32), TABLE_ROWS, axis=1), BF16)
    tok3 = pl.BlockSpec((tt, SUBLANES, LANES), lambda i: (i, 0, 0))
    col = pl.BlockSpec((n_sel, tt), lambda i: (0, i))
    ids = pl.BlockSpec((tt, n_sel), lambda i: (i, 0), memory_space=pltpu.SMEM)
    coef = pl.pallas_call(
        _peer_u_kernel,
        grid=(n // tt,),
        in_specs=[ids, tok3, col, _vmem_full(), _vmem_full()],
        out_specs=col,
        out_shape=jax.ShapeDtypeStruct((n_sel, n), F32),
        scratch_shapes=[pltpu.VMEM((n_sel * TABLE_ROWS, grp * LANES), F32),
                        pltpu.VMEM((n_sel * TABLE_ROWS, tt), F32)],
        compiler_params=_cparams("parallel"),
        name="peer_u",
    )(eid, hn.reshape(n, SUBLANES, LANES), g_t, wts["peer_u"], fold)
    out = pl.pallas_call(
        _peer_v_kernel,
        grid=(n // tt,),
        in_specs=[ids, col, tok3, _vmem_full()],
        out_specs=tok3,
        out_shape=jax.ShapeDtypeStruct((n, SUBLANES, LANES), F32),
        scratch_shapes=[pltpu.VMEM((2, n_sel, LANES), F32)],
        compiler_params=_cparams("parallel"),
        name="peer_v",
    )(eid, coef, x1.reshape(n, SUBLANES, LANES), wts["peer_v"])
    return out.reshape(n, D_MODEL)


IN_WIDTHS = (D_INNER, CONV_DIM, SSM_HEADS, D_MODEL, KV_DIM, KV_DIM, D_MODEL, N_BRANCHES * D_MODEL)


def _row(v):
    return v.reshape(1, -1).astype(F32)


def _prepare(p):
    offs = np.cumsum((0,) + IN_WIDTHS)
    seg = [p["w_in"][:, offs[i]:offs[i + 1]].astype(BF16) for i in range(len(IN_WIDTHS))]
    wz, wxbc, wdt, wq, wk, wv, wqm, wg = seg
    wdt = jnp.pad(wdt, ((0, 0), (0, LANES - SSM_HEADS)))
    e64, et64 = _group_indicators(D_MODEL, HEAD_DIM)
    e64k, et64k = _group_indicators(KV_DIM, HEAD_DIM)
    e256, et256 = _group_indicators(D_MODEL, MEM_HEAD_DIM)
    gq = _row(jnp.tile(p["g_q"], MOBA_HEADS)) * (HEAD_DIM ** -0.5 * LOG2E)
    gk = _row(jnp.tile(p["g_k"], MOBA_KV_HEADS))
    gmq = _row(jnp.tile(p["g_mq"], MEM_HEADS)) * MEM_HEAD_DIM ** -0.5
    inproj = [_row(p["g_mix_norm"]), wz, wxbc, wdt, wq, wk, wv, wqm, wg, _row(p["b_gate"]),
              gq, gk, gmq, e64, et64, e64k, et64k, e256, et256]

    pad_h = lambda v: jnp.pad(_row(v), ((0, 0), (0, LANES - SSM_HEADS)))
    e2 = np.zeros((LANES, D_INNER), np.float32)
    e2[np.arange(D_INNER) // SSM_HEAD_DIM, np.arange(D_INNER)] = 1.0
    e2 = jnp.asarray(np.concatenate([e2, e2], axis=0), BF16)
    eg, etg2 = _group_indicators(D_INNER, D_INNER // SSM_GROUPS)
    ssd = [p["conv_w"].astype(F32), _row(p["conv_b"]), pad_h(p["dt_bias"]), pad_h(p["a_log"]),
           _row(jnp.repeat(p["d_skip"], SSM_HEAD_DIM)), _row(p["g_ssm_norm"]), e2, eg, etg2]

    memkv = [_row(p["g_mem_norm"]), p["w_mem_kv"].astype(BF16),
             _row(jnp.tile(p["g_mk"], MEM_HEADS)), e256, et256]
    merge = [p["w_ssm_out"].astype(BF16), p["w_moba_out"].astype(BF16), p["w_mem_out"].astype(BF16),
             p["w_out"].astype(BF16), _row(p["g_ffn_norm"])]
    keys = p["peer_keys"].astype(BF16).reshape(PEER_HEADS * 2, PEER_KEYS, PEER_HALF)
    peer_topk = [p["w_pq"].astype(BF16), jnp.concatenate([keys, keys], axis=-1)]
    return {"inproj": inproj, "ssd": ssd, "memkv": memkv, "merge": merge, "peer_topk": peer_topk,
            "peer_u": _pack_table(p["peer_u"]), "peer_v": _pack_table(p["peer_v"])}


def _tile(n, pref):
    return pref if n % pref == 0 else n


def _group_forward(x, wts, ssd_state, attn_fn, mem_k, mem_v):
    b, t, _ = x.shape
    n = b * t
    x2 = x.reshape(n, D_MODEL)
    z, xbc, dt, q, k, v, qm, gate = _inproj(x2, wts, _tile(n, 256))
    conv0, ssm0, q_rows, n_valid = ssd_state
    y_ssd, ssm_new, conv_new = _ssd(xbc.reshape(b, t, CONV_DIM), z.reshape(b, t, D_INNER),
                                    dt.reshape(b, t, LANES), conv0, ssm0, wts, q_rows, n_valid)
    k3, v3 = k.reshape(b, t, KV_DIM), v.reshape(b, t, KV_DIM)
    y_moba = attn_fn(q.reshape(b, t, D_MODEL), k3, v3)
    y_mem = _memattn(qm.reshape(b, t, D_MODEL), mem_k, mem_v, _tile(t, 512))
    x1, hn = _merge(x2, y_ssd.reshape(n, D_INNER), y_moba.reshape(n, D_MODEL),
                    y_mem.reshape(n, D_MODEL), gate, wts, _tile(n, 256))
    out = _peer(x1, hn, wts, _tile(n, 256), _tile(n, 128))
    return (out.reshape(b, t, D_MODEL),
            ssm_new.reshape(b, SSM_HEADS, SSM_HEAD_DIM, D_STATE), conv_new,
            k3.reshape(b, t, MOBA_KV_HEADS, HEAD_DIM), v3.reshape(b, t, MOBA_KV_HEADS, HEAD_DIM))


def kernel(x_prompt, x_sample, cache_k, cache_v, state_ssm, state_conv, cache_mem_k, cache_mem_v,
           page_table, mem_prompt, g_mix_norm, w_in, conv_w, conv_b, dt_bias, a_log, d_skip,
           g_ssm_norm, w_ssm_out, g_q, g_k, rel_bias, w_moba_out, g_mem_norm, w_mem_kv, g_mq, g_mk,
           w_mem_out, b_gate, w_out, g_ffn_norm, w_pq, peer_keys, peer_u, peer_v):
    depth = w_in.shape[0]
    bp, tp, _ = x_prompt.shape
    sb, dec_seq, _ = x_sample.shape
    mem_len = mem_prompt.shape[1]
    n_pool, page = cache_k.shape[1], cache_k.shape[2]
    past_len = page_table.shape[1] * page
    assert past_len % MOBA_BLOCK == 0 and tp % MOBA_BLOCK == 0

    far_d = _far_distance()
    n_near_p = -(-(far_d + MOBA_BLOCK - 1) // MOBA_BLOCK) - 1
    bias_p = _bias_prompt(rel_bias, 2 * n_near_p + 2)
    farb_p = _row(jnp.repeat(rel_bias[NUM_BUCKETS - 1], Q_BLOCK)) * LOG2E
    nbp = past_len // MOBA_BLOCK
    n_far_s = min(max((past_len - (MOBA_BLOCK - 1) - far_d) // MOBA_BLOCK + 1, 0), nbp)
    bias_s = _bias_sample(rel_bias, past_len, n_far_s, nbp - n_far_s, dec_seq)
    farb_s = _row(jnp.repeat(rel_bias[NUM_BUCKETS - 1], dec_seq)) * LOG2E

    per_layer = dict(g_mix_norm=g_mix_norm, w_in=w_in, conv_w=conv_w, conv_b=conv_b, dt_bias=dt_bias,
                     a_log=a_log, d_skip=d_skip, g_ssm_norm=g_ssm_norm, w_ssm_out=w_ssm_out, g_q=g_q,
                     g_k=g_k, w_moba_out=w_moba_out, g_mem_norm=g_mem_norm, w_mem_kv=w_mem_kv,
                     g_mq=g_mq, g_mk=g_mk, w_mem_out=w_mem_out, b_gate=b_gate, w_out=w_out,
                     g_ffn_norm=g_ffn_norm, w_pq=w_pq, peer_keys=peer_keys, peer_u=peer_u,
                     peer_v=peer_v)
    xp, xs = x_prompt, x_sample
    outs = [[] for _ in range(10)]
    chunk = math.gcd(tp, SSD_CHUNK)
    for l in range(depth):
        wts = _prepare({name: val[l] for name, val in per_layer.items()})
        mk, mv = _memkv(mem_prompt.reshape(bp * mem_len, D_MODEL), wts)
        mk, mv = mk.reshape(bp, mem_len, D_MODEL), mv.reshape(bp, mem_len, D_MODEL)
        prompt_attn = functools.partial(_moba_prompt, bias_tiles=bias_p, farb=farb_p, n_near=n_near_p)
        xp, h_p, c_p, k_p, v_p = _group_forward(xp, wts, (None, None, chunk, chunk), prompt_attn, mk, mv)
        sample_attn = functools.partial(
            _moba_sample, cache_k=cache_k[l], cache_v=cache_v[l], page_table=page_table,
            bias_tiles=bias_s, farb=farb_s, n_far=n_far_s, pages_per_step=8)
        conv0 = jnp.pad(state_conv[l], ((0, 0), (SUBLANES - (CONV_WIDTH - 1), 0), (0, 0)))
        ssm0 = state_ssm[l].reshape(sb, D_INNER, D_STATE)
        xs, h_s, c_s, k_s, v_s = _group_forward(
            xs, wts, (conv0, ssm0, Q_BLOCK, dec_seq), sample_attn,
            cache_mem_k[l].reshape(sb, mem_len, D_MODEL), cache_mem_v[l].reshape(sb, mem_len, D_MODEL))
        heads = (MEM_HEADS, MEM_HEAD_DIM)
        for lst, val in zip(outs, (k_p, v_p, k_s, v_s, h_p, h_s, c_p, c_s,
                                   mk.reshape(bp, mem_len, *heads), mv.reshape(bp, mem_len, *heads))):
            lst.append(val)
    return (xp, xs) + tuple(jnp.stack(lst) for lst in outs)
```

```python
import functools
import math

import numpy as np
import jax
import jax.numpy as jnp
from jax import lax
from jax.experimental import pallas as pl
from jax.experimental.pallas import tpu as pltpu

F32 = jnp.float32
BF16 = jnp.bfloat16
I32 = jnp.int32
NEG_INF = float("-inf")

D_MODEL = 1024
D_INNER = 2048
SSM_HEAD_DIM = 64
SSM_HEADS = 32
SSM_GROUPS = 4
D_STATE = 128
CONV_WIDTH = 4
CONV_DIM = D_INNER + 2 * SSM_GROUPS * D_STATE
SSD_CHUNK = 256
HEAD_DIM = 64
MOBA_HEADS = 16
MOBA_KV_HEADS = 4
Q_PER_KV = 4
KV_DIM = MOBA_KV_HEADS * HEAD_DIM
MOBA_BLOCK = 256
MOBA_TOPK = 3
Q_BLOCK = 128
NUM_BUCKETS = 32
MAX_DISTANCE = 2048
MEM_HEADS = 4
MEM_HEAD_DIM = 256
PEER_HEADS = 8
PEER_KEYS = 128
PEER_HALF = 128
PEER_TOPK = 16
N_BRANCHES = 3
EPS = 1e-6
LOG2E = math.log2(math.e)

LANES = 128
SUBLANES = 8
VMEM_LIMIT_BYTES = 56 * 1024 * 1024


def _cparams(*sem):
    return pltpu.CompilerParams(dimension_semantics=sem, vmem_limit_bytes=VMEM_LIMIT_BYTES)


def _vmem_full():
    return pl.BlockSpec(memory_space=pltpu.VMEM)


def _dot(a, b):
    return jnp.dot(a, b, preferred_element_type=F32)


def _split_bf16(x):
    hi = x.astype(BF16)
    lo = (x - hi.astype(F32)).astype(BF16)
    return hi, lo


def _group_indicators(dim, gsize):
    e = np.zeros((dim, LANES), np.float32)
    e[np.arange(dim), np.arange(dim) // gsize] = 1.0
    et2 = np.concatenate([e.T, e.T], axis=0)
    return jnp.asarray(e, BF16), jnp.asarray(et2, BF16)


def _group_rr(x, e_ref, et2_ref, gsize):
    ssq = _dot((x * x).astype(BF16), e_ref[...])
    r = lax.rsqrt(ssq * (1.0 / gsize) + EPS)
    r_hi, r_lo = _split_bf16(r)
    return _dot(jnp.concatenate([r_hi, r_lo], axis=1), et2_ref[...])


def _inproj_kernel(x_ref, gmix_ref, wz_ref, wxbc_ref, wdt_ref, wq_ref, wk_ref, wv_ref, wqm_ref,
                   wg_ref, bgate_ref, gq_ref, gk_ref, gmq_ref, e64_ref, et64_ref, e64k_ref,
                   et64k_ref, e256_ref, et256_ref,
                   z_ref, xbc_ref, dt_ref, q_ref, k_ref, v_ref, qm_ref, gate_ref):
    x = x_ref[...]
    h = (x * lax.rsqrt(jnp.mean(x * x, axis=-1, keepdims=True) + EPS) * gmix_ref[...]).astype(BF16)
    z_ref[...] = _dot(h, wz_ref[...]).astype(BF16)
    xbc_ref[...] = _dot(h, wxbc_ref[...])
    dt_ref[...] = _dot(h, wdt_ref[...])
    q = _dot(h, wq_ref[...])
    q_ref[...] = (q * _group_rr(q, e64_ref, et64_ref, HEAD_DIM) * gq_ref[...]).astype(BF16)
    k = _dot(h, wk_ref[...])
    k_ref[...] = k * _group_rr(k, e64k_ref, et64k_ref, HEAD_DIM) * gk_ref[...]
    v_ref[...] = _dot(h, wv_ref[...])
    qm = _dot(h, wqm_ref[...])
    qm_ref[...] = (qm * _group_rr(qm, e256_ref, et256_ref, MEM_HEAD_DIM) * gmq_ref[...]).astype(BF16)
    gate_ref[...] = jax.nn.sigmoid(_dot(h, wg_ref[...]) + bgate_ref[...]).astype(BF16)


def _inproj(x, wts, tm):
    n = x.shape[0]
    assert n % tm == 0
    consts = wts["inproj"]
    row = lambda w: pl.BlockSpec((tm, w), lambda i: (i, 0))
    out_shape = (
        jax.ShapeDtypeStruct((n, D_INNER), BF16),
        jax.ShapeDtypeStruct((n, CONV_DIM), F32),
        jax.ShapeDtypeStruct((n, LANES), F32),
        jax.ShapeDtypeStruct((n, D_MODEL), BF16),
        jax.ShapeDtypeStruct((n, KV_DIM), F32),
        jax.ShapeDtypeStruct((n, KV_DIM), F32),
        jax.ShapeDtypeStruct((n, D_MODEL), BF16),
        jax.ShapeDtypeStruct((n, N_BRANCHES * D_MODEL), BF16),
    )
    return pl.pallas_call(
        _inproj_kernel,
        grid=(n // tm,),
        in_specs=[row(D_MODEL)] + [_vmem_full()] * len(consts),
        out_specs=tuple(row(s.shape[1]) for s in out_shape),
        out_shape=out_shape,
        compiler_params=_cparams("parallel"),
        name="inproj",
    )(x, *consts)


def _ssd_kernel(*refs, q_rows, n_valid, has_init):
    if has_init:
        (xbc_ref, z_ref, dt_ref, conv0_ref, ssm0_ref, cw_ref, cb_ref, dtb_ref, alog_ref, dskip_ref,
         gssm_ref, e2_ref, eg_ref, etg2_ref, y_ref, ssm_out_ref, conv_out_ref,
         xw_ref, st_ref, yb_ref) = refs
    else:
        (xbc_ref, z_ref, dt_ref, cw_ref, cb_ref, dtb_ref, alog_ref, dskip_ref,
         gssm_ref, e2_ref, eg_ref, etg2_ref, y_ref, ssm_out_ref, conv_out_ref,
         xw_ref, st_ref, yb_ref) = refs
    c = pl.program_id(1)
    last = pl.num_programs(1) - 1
    Q, NV = q_rows, n_valid
    G = SSM_GROUPS
    GW = D_INNER // G

    @pl.when(c == 0)
    def _init():
        if has_init:
            xw_ref[0:SUBLANES, :] = conv0_ref[0]
            st_ref[...] = ssm0_ref[0].T
        else:
            xw_ref[0:SUBLANES, :] = jnp.zeros((SUBLANES, CONV_DIM), F32)
            st_ref[...] = jnp.zeros_like(st_ref)

    xw_ref[SUBLANES:SUBLANES + NV, :] = xbc_ref[0]
    if NV < Q:
        xw_ref[SUBLANES + NV:SUBLANES + Q, :] = jnp.zeros((Q - NV, CONV_DIM), F32)

    acc = jnp.broadcast_to(cb_ref[...], (Q, CONV_DIM))
    for kk in range(CONV_WIDTH):
        off = SUBLANES - (CONV_WIDTH - 1) + kk
        acc = acc + cw_ref[kk:kk + 1, :] * xw_ref[off:off + Q, :]
    xc = jax.nn.silu(acc)
    xs = xc[:, :D_INNER]
    bm = xc[:, D_INNER:D_INNER + G * D_STATE]
    cm = xc[:, D_INNER + G * D_STATE:]

    dtv = dt_ref[0]
    if NV < Q:
        dtv = jnp.concatenate([dtv, jnp.zeros((Q - NV, LANES), F32)], axis=0)
    xdt_pre = dtv + dtb_ref[...]
    dt = jnp.maximum(xdt_pre, 0.0) + jnp.log1p(jnp.exp(-jnp.abs(xdt_pre)))
    row_i = lax.broadcasted_iota(I32, (Q, LANES), 0)
    if NV < Q:
        dt = jnp.where(row_i < NV, dt, 0.0)
    a = -jnp.exp(alog_ref[...])
    da = dt * a

    ri = lax.broadcasted_iota(I32, (Q, Q), 0)
    ci = lax.broadcasted_iota(I32, (Q, Q), 1)
    causal = ri >= ci
    lower = causal.astype(F32)
    upper = (ri <= ci).astype(F32)
    cum = jnp.dot(lower, da, preferred_element_type=F32, precision=lax.Precision.HIGHEST)
    cum_t = jnp.dot(da.T, upper, preferred_element_type=F32, precision=lax.Precision.HIGHEST)

    def expand(v):
        hi, lo = _split_bf16(v)
        return _dot(jnp.concatenate([hi, lo], axis=1), e2_ref[...])

    dt_x = expand(dt)
    ecum_x = expand(jnp.exp(cum))
    toend_x = expand(jnp.exp(cum[Q - 1:Q, :] - cum))
    xdt = xs * dt_x
    xdt_b = xdt.astype(BF16)
    xw_b = (xdt * toend_x).astype(BF16)
    dec_row = ecum_x[Q - 1:Q, :]
    lane_lo = lax.broadcasted_iota(I32, (Q, LANES), 1) < SSM_HEAD_DIM

    for g in range(G):
        cg = cm[:, g * D_STATE:(g + 1) * D_STATE].astype(BF16)
        bg = bm[:, g * D_STATE:(g + 1) * D_STATE]
        bg_b = bg.astype(BF16)
        cb = lax.dot_general(cg, bg_b, (((1,), (1,)), ((), ())), preferred_element_type=F32)
        st_g = st_ref[:, g * GW:(g + 1) * GW]
        y_off = _dot(cg, st_g.astype(BF16))
        for pair in range(GW // LANES):
            col0 = g * GW + pair * LANES
            x2 = xdt_b[:, col0:col0 + LANES]
            halves = []
            for e in range(2):
                hh = col0 // SSM_HEAD_DIM + e
                seg = cum[:, hh:hh + 1] - cum_t[hh:hh + 1, :]
                lmat = jnp.exp(jnp.where(causal, seg, NEG_INF))
                halves.append(_dot((cb * lmat).astype(BF16), x2))
            yb_ref[:, col0:col0 + LANES] = jnp.where(lane_lo, halves[0], halves[1])
        yb_ref[:, g * GW:(g + 1) * GW] = (yb_ref[:, g * GW:(g + 1) * GW]
                                          + y_off * ecum_x[:, g * GW:(g + 1) * GW])
        st_ref[:, g * GW:(g + 1) * GW] = (st_g * dec_row[:, g * GW:(g + 1) * GW]
                                          + _dot(bg.T.astype(BF16), xw_b[:, g * GW:(g + 1) * GW]))

    y = yb_ref[...] + xs * dskip_ref[...]
    zf = z_ref[0].astype(F32)
    if NV < Q:
        zf = jnp.concatenate([zf, jnp.zeros((Q - NV, D_INNER), F32)], axis=0)
    yz = y * jax.nn.silu(zf)
    yn = yz * _group_rr(yz, eg_ref, etg2_ref, GW) * gssm_ref[...]
    y_ref[0] = yn[0:NV, :].astype(BF16)

    xw_ref[0:SUBLANES, :] = xw_ref[NV:NV + SUBLANES, :]

    @pl.when(c == last)
    def _fin():
        ssm_out_ref[0] = st_ref[...].T
        conv_out_ref[0] = xw_ref[SUBLANES - (CONV_WIDTH - 1):SUBLANES, :]


def _ssd(xbc, z, dt, conv0, ssm0, wts, q_rows, n_valid):
    b, t, _ = xbc.shape
    assert t % n_valid == 0 and n_valid >= CONV_WIDTH - 1
    nc = t // n_valid
    has_init = conv0 is not None
    consts = wts["ssd"]
    tile = lambda w: pl.BlockSpec((1, n_valid, w), lambda i, c: (i, c, 0))
    in_specs = [tile(CONV_DIM), tile(D_INNER), tile(LANES)]
    args = [xbc, z, dt]
    if has_init:
        in_specs += [pl.BlockSpec((1, SUBLANES, CONV_DIM), lambda i, c: (i, 0, 0)),
                     pl.BlockSpec((1, D_INNER, D_STATE), lambda i, c: (i, 0, 0))]
        args += [conv0, ssm0]
    in_specs += [_vmem_full()] * len(consts)
    out_shape = (jax.ShapeDtypeStruct((b, t, D_INNER), BF16),
                 jax.ShapeDtypeStruct((b, D_INNER, D_STATE), F32),
                 jax.ShapeDtypeStruct((b, CONV_WIDTH - 1, CONV_DIM), F32))
    out_specs = (tile(D_INNER),
                 pl.BlockSpec((1, D_INNER, D_STATE), lambda i, c: (i, 0, 0)),
                 pl.BlockSpec((1, CONV_WIDTH - 1, CONV_DIM), lambda i, c: (i, 0, 0)))
    return pl.pallas_call(
        functools.partial(_ssd_kernel, q_rows=q_rows, n_valid=n_valid, has_init=has_init),
        grid=(b, nc),
        in_specs=in_specs,
        out_specs=out_specs,
        out_shape=out_shape,
        scratch_shapes=[pltpu.VMEM((SUBLANES + q_rows, CONV_DIM), F32),
                        pltpu.VMEM((D_STATE, D_INNER), F32),
                        pltpu.VMEM((q_rows, D_INNER), F32)],
        compiler_params=_cparams("parallel", "arbitrary"),
        name="ssd_scan",
    )(*args, *consts)


def _rel_bucket(dist):
    n = jnp.maximum(dist, 0)
    max_exact = NUM_BUCKETS // 2
    nf = jnp.maximum(n, 1).astype(F32)
    large = max_exact + (jnp.log(nf / max_exact) / math.log(MAX_DISTANCE / max_exact)
                         * (NUM_BUCKETS - max_exact)).astype(I32)
    large = jnp.minimum(large, NUM_BUCKETS - 1)
    return jnp.where(n < max_exact, n, large)


def _far_distance():
    d = np.arange(1, 4 * MAX_DISTANCE, dtype=np.float64)
    large = 16 + np.floor(np.log(d / 16) / math.log(MAX_DISTANCE / 16) * 16)
    below = np.nonzero(large < NUM_BUCKETS - 1)[0]
    return int(d[below[-1]]) + 1 + 1


def _bias_from_dist(dist, rbl_ref):
    bucket = _rel_bucket(dist)
    val = jnp.zeros(dist.shape, F32)
    for b in range(NUM_BUCKETS):
        val = jnp.where(bucket == b, rbl_ref[b:b + 1, :], val)
    return jnp.where(dist >= 0, val * LOG2E, NEG_INF)


def _bias_prompt_kernel(rbl_ref, o_ref):
    di = pl.program_id(0)
    w = Q_PER_KV * Q_BLOCK
    j = lax.broadcasted_iota(I32, (MOBA_BLOCK, w), 0)
    i = lax.broadcasted_iota(I32, (MOBA_BLOCK, w), 1) % Q_BLOCK
    o_ref[0, 0] = _bias_from_dist(di * Q_BLOCK + i - j, rbl_ref.at[0]).astype(BF16)


def _bias_prompt(rel_bias, n_tiles):
    w = Q_PER_KV * Q_BLOCK
    rbl = jnp.repeat(rel_bias.reshape(NUM_BUCKETS, MOBA_KV_HEADS, Q_PER_KV).transpose(1, 0, 2),
                     Q_BLOCK, axis=2)
    return pl.pallas_call(
        _bias_prompt_kernel,
        grid=(n_tiles, MOBA_KV_HEADS),
        in_specs=[pl.BlockSpec((1, NUM_BUCKETS, w), lambda d, h: (h, 0, 0))],
        out_specs=pl.BlockSpec((1, 1, MOBA_BLOCK, w), lambda d, h: (d, h, 0, 0)),
        out_shape=jax.ShapeDtypeStruct((n_tiles, MOBA_KV_HEADS, MOBA_BLOCK, w), BF16),
        compiler_params=_cparams("parallel", "parallel"),
        name="moba_bias_prompt",
    )(rbl)


def _bias_sample_kernel(rbl_ref, o_ref, *, past_len, first_block, n_near, dec_seq):
    r = pl.program_id(0)
    kpos0 = jnp.where(r < n_near, (first_block + r) * MOBA_BLOCK, past_len)
    j = lax.broadcasted_iota(I32, (MOBA_BLOCK, LANES), 0)
    t = lax.broadcasted_iota(I32, (MOBA_BLOCK, LANES), 1) % dec_seq
    o_ref[0] = _bias_from_dist(past_len + t - (kpos0 + j), rbl_ref)


def _bias_sample(rel_bias, past_len, first_block, n_near, dec_seq):
    rbl = jnp.repeat(rel_bias, dec_seq, axis=1)
    return pl.pallas_call(
        functools.partial(_bias_sample_kernel, past_len=past_len, first_block=first_block,
                          n_near=n_near, dec_seq=dec_seq),
        grid=(n_near + 1,),
        in_specs=[_vmem_full()],
        out_specs=pl.BlockSpec((1, MOBA_BLOCK, LANES), lambda r: (r, 0, 0)),
        out_shape=jax.ShapeDtypeStruct((n_near + 1, MOBA_BLOCK, LANES), F32),
        compiler_params=_cparams("parallel"),
        name="moba_bias_sample",
    )(rbl)


def _kmean_kernel(k_ref, o_ref, *, nb):
    for b in range(nb):
        blk = k_ref[0, b * MOBA_BLOCK:(b + 1) * MOBA_BLOCK, :]
        o_ref[0, b:b + 1, :] = jnp.sum(blk, axis=0, keepdims=True) * (1.0 / MOBA_BLOCK)


def _kmean(k):
    b, t, _ = k.shape
    nb = t // MOBA_BLOCK
    return pl.pallas_call(
        functools.partial(_kmean_kernel, nb=nb),
        grid=(b,),
        in_specs=[pl.BlockSpec((1, t, KV_DIM), lambda i: (i, 0, 0))],
        out_specs=pl.BlockSpec((1, nb, KV_DIM), lambda i: (i, 0, 0)),
        out_shape=jax.ShapeDtypeStruct((b, nb, KV_DIM), F32),
        compiler_params=_cparams("parallel"),
        name="moba_kmean",
    )(k)


def _select_blocks(gate, own, n_sel_rows):
    nb = gate.shape[0]
    blk = lax.broadcasted_iota(I32, gate.shape, 0)
    sel = jnp.zeros(gate.shape, F32)
    for t in range(MOBA_TOPK):
        m = jnp.max(gate, axis=0, keepdims=True)
        pos = jnp.min(jnp.where(gate == m, blk, nb), axis=0, keepdims=True)
        hit = blk == pos
        sel = jnp.where(hit, jnp.maximum(sel, jnp.where(t < n_sel_rows, 1.0, 0.0)), sel)
        gate = jnp.where(hit, NEG_INF, gate)
    return sel, blk


V_ROWS = HEAD_DIM + 16


def _moba_prompt_kernel(qt_ref, k_ref, vt_ref, km_ref, bias_ref, farb_ref, o_ref,
                        qx_ref, add_ref, m_ref, acc_ref, s_ref, *, n_near):
    a = pl.program_id(1)
    own = a // 2
    par = a % 2
    nb = k_ref.shape[1]
    hw = Q_PER_KV * Q_BLOCK
    w = MOBA_KV_HEADS * hw

    row_h = lax.broadcasted_iota(I32, (KV_DIM, Q_BLOCK), 0) // HEAD_DIM
    for h in range(MOBA_KV_HEADS):
        for g in range(Q_PER_KV):
            piece = jnp.where(row_h == h, qt_ref[0, g], jnp.zeros((), BF16))
            col = (h * Q_PER_KV + g) * Q_BLOCK
            qx_ref[:, col:col + Q_BLOCK] = piece

    km_hi, km_lo = _split_bf16(km_ref[0])
    gate = _dot(km_hi, qx_ref[...]) + _dot(km_lo, qx_ref[...])
    blk0 = lax.broadcasted_iota(I32, (nb, w), 0)
    gate = jnp.where(blk0 < own, gate, NEG_INF)
    sel, blk = _select_blocks(gate, own, own)
    far = (own - blk) > n_near
    add = jnp.where(sel > 0.0, jnp.where(far, farb_ref[...], 0.0), NEG_INF)
    add_ref[...] = jnp.where(blk == own, 0.0, add)

    m_ref[...] = jnp.full(m_ref.shape, NEG_INF, F32)
    acc_ref[...] = jnp.zeros(acc_ref.shape, F32)

    def scores(b, s_out):
        kb = k_ref[0, b]
        for h in range(MOBA_KV_HEADS):
            s_out[h] = _dot(kb, qx_ref[:, h * hw:(h + 1) * hw])

    def attend(b, s_in, di):
        for h in range(MOBA_KV_HEADS):
            s = s_in[h]
            if di is not None:
                s = s + bias_ref[di, h].astype(F32)
            add = add_ref[pl.ds(b, 1), h * hw:(h + 1) * hw]
            m_prev = m_ref[h]
            m_new = jnp.maximum(m_prev, jnp.max(s, axis=0, keepdims=True) + add)
            p = jnp.exp2(s - (m_new - add))
            alpha = jnp.exp2(m_prev - m_new)
            vth = vt_ref[0, b, h * V_ROWS:(h + 1) * V_ROWS, :]
            acc_ref[h] = alpha * acc_ref[h] + _dot(vth, p.astype(BF16))
            m_ref[h] = m_new

    n_biased = jnp.minimum(n_near, own) + 1

    def near_body(idx, carry):
        scores(own - idx, s_ref)
        attend(own - idx, s_ref, par + 2 * idx)
        return carry

    lax.fori_loop(0, n_biased, near_body, 0)

    def far_body(idx, carry):
        scores(own - idx, s_ref)
        attend(own - idx, s_ref, None)
        return carry

    lax.fori_loop(n_biased, own + 1, far_body, 0)

    for h in range(MOBA_KV_HEADS):
        acc = acc_ref[h]
        o_ref[0, h] = (acc[0:HEAD_DIM, :] / acc[HEAD_DIM:HEAD_DIM + 1, :]).astype(BF16)


def _moba_prompt(q, k, v, bias_tiles, farb, n_near):
    b, t, _ = q.shape
    assert t % MOBA_BLOCK == 0
    nb, nq = t // MOBA_BLOCK, t // Q_BLOCK
    hw = Q_PER_KV * Q_BLOCK
    w = MOBA_KV_HEADS * hw
    km = _kmean(k)
    qt = q.reshape(b, t, MOBA_KV_HEADS, Q_PER_KV, HEAD_DIM).transpose(0, 3, 2, 4, 1).reshape(
        b, Q_PER_KV, KV_DIM, t)
    kb = k.astype(BF16).reshape(b, nb, MOBA_BLOCK, KV_DIM)
    vt = v.astype(BF16).reshape(b, nb, MOBA_BLOCK, MOBA_KV_HEADS, HEAD_DIM).transpose(0, 1, 3, 4, 2)
    ones_rows = jnp.zeros((V_ROWS - HEAD_DIM, MOBA_BLOCK), BF16).at[0].set(1.0)
    vt = jnp.concatenate(
        [vt, jnp.broadcast_to(ones_rows, (b, nb, MOBA_KV_HEADS) + ones_rows.shape)], axis=3)
    vt = vt.reshape(b, nb, MOBA_KV_HEADS * V_ROWS, MOBA_BLOCK)
    out = pl.pallas_call(
        functools.partial(_moba_prompt_kernel, n_near=n_near),
        grid=(b, nq),
        in_specs=[pl.BlockSpec((1, Q_PER_KV, KV_DIM, Q_BLOCK), lambda i, a: (i, 0, 0, a)),
                  pl.BlockSpec((1, nb, MOBA_BLOCK, KV_DIM), lambda i, a: (i, 0, 0, 0)),
                  pl.BlockSpec((1, nb, MOBA_KV_HEADS * V_ROWS, MOBA_BLOCK), lambda i, a: (i, 0, 0, 0)),
                  pl.BlockSpec((1, nb, KV_DIM), lambda i, a: (i, 0, 0)),
                  _vmem_full(), _vmem_full()],
        out_specs=pl.BlockSpec((1, MOBA_KV_HEADS, HEAD_DIM, hw), lambda i, a: (i, 0, 0, a)),
        out_shape=jax.ShapeDtypeStruct((b, MOBA_KV_HEADS, HEAD_DIM, nq * hw), BF16),
        scratch_shapes=[pltpu.VMEM((KV_DIM, w), BF16),
                        pltpu.VMEM((nb, w), F32),
                        pltpu.VMEM((MOBA_KV_HEADS, 1, hw), F32),
                        pltpu.VMEM((MOBA_KV_HEADS, V_ROWS, hw), F32),
                        pltpu.VMEM((MOBA_KV_HEADS, MOBA_BLOCK, hw), F32)],
        compiler_params=_cparams("parallel", "arbitrary"),
        name="moba_prompt",
    )(qt, kb, vt, km, bias_tiles, farb)
    out = out.reshape(b, MOBA_KV_HEADS, HEAD_DIM, nq, Q_PER_KV, Q_BLOCK).transpose(0, 3, 5, 1, 4, 2)
    return out.reshape(b, t, D_MODEL)


def _moba_sample_kernel(pt_ref, qx_ref, *rest, pages_per_step, n_far, dec_seq):
    del pt_ref
    gp = pages_per_step
    k_refs, v_refs = rest[:gp], rest[gp:2 * gp]
    (knew_ref, vnew_ref, biass_ref, farb_ref, o_ref,
     st_ref, vt_ref, gs_ref, add_ref) = rest[2 * gp:]
    tn = (((0,), (0,)), ((), ()))
    j = pl.program_id(1)
    nbp = st_ref.shape[0]
    n_near = nbp - n_far
    page = MOBA_BLOCK // 2
    qx = qx_ref[0]

    for i in range(gp):
        blk = j * (gp // 2) + i // 2
        half = i % 2
        keys = slice(half * page, (half + 1) * page)
        s = lax.dot_general(k_refs[i][0].astype(BF16), qx, tn, preferred_element_type=F32)
        st_ref[blk, keys, :] = s
        vt_ref[blk, :, keys] = v_refs[i][0].astype(BF16)
        ssum = jnp.sum(s, axis=0, keepdims=True)
        if half == 0:
            gs_ref[pl.ds(blk, 1), :] = ssum
        else:
            gs_ref[pl.ds(blk, 1), :] = gs_ref[pl.ds(blk, 1), :] + ssum

    @pl.when(j == pl.num_programs(1) - 1)
    def _fin():
        gate = gs_ref[...] * (1.0 / MOBA_BLOCK)
        sel, blk = _select_blocks(gate, nbp, nbp)
        add_ref[...] = jnp.where(sel > 0.0, jnp.where(blk < n_far, farb_ref[...], 0.0), NEG_INF)

        s_own = _dot(knew_ref[0].astype(BF16), qx) + biass_ref[n_near, 0:Q_BLOCK, :]
        m = jnp.max(s_own, axis=0, keepdims=True)

        def far_logits(b, m):
            lg = st_ref[b] + add_ref[pl.ds(b, 1), :]
            st_ref[b] = lg
            return jnp.maximum(m, jnp.max(lg, axis=0, keepdims=True))

        m = lax.fori_loop(0, n_far, far_logits, m)
        for r in range(n_near):
            b = n_far + r
            lg = st_ref[b] + add_ref[b:b + 1, :] + biass_ref[r]
            st_ref[b] = lg
            m = jnp.maximum(m, jnp.max(lg, axis=0, keepdims=True))

        p_own = jnp.exp2(s_own - m)
        l0 = jnp.sum(p_own, axis=0, keepdims=True)
        acc0 = _dot(vnew_ref[0].T.astype(BF16), p_own.astype(BF16))

        unroll = math.gcd(nbp, 4)

        def weighted(i, carry):
            l, acc = carry
            for u in range(unroll):
                b = i * unroll + u
                p = jnp.exp2(st_ref[b] - m)
                l = l + jnp.sum(p, axis=0, keepdims=True)
                acc = acc + _dot(vt_ref[b], p.astype(BF16))
            return l, acc

        l, acc = lax.fori_loop(0, nbp // unroll, weighted, (l0, acc0))
        y = acc / l
        lane_h = lax.broadcasted_iota(I32, (HEAD_DIM, LANES), 1) // (Q_PER_KV * dec_seq)
        out = jnp.zeros((HEAD_DIM, LANES), F32)
        for h in range(MOBA_KV_HEADS):
            out = jnp.where(lane_h == h, y[h * HEAD_DIM:(h + 1) * HEAD_DIM, :], out)
        o_ref[0] = out


def _moba_sample(q, k_new, v_new, cache_k, cache_v, page_table, bias_tiles, farb, n_far,
                 pages_per_step):
    s, dec_seq, _ = q.shape
    n_pool, page = cache_k.shape[:2]
    n_pages = page_table.shape[1]
    assert page * 2 == MOBA_BLOCK and n_pages % pages_per_step == 0 and pages_per_step % 2 == 0
    assert MOBA_HEADS * dec_seq == LANES and dec_seq <= Q_BLOCK
    nbp = n_pages // 2
    gp = pages_per_step
    q5 = q.reshape(s, dec_seq, MOBA_KV_HEADS, Q_PER_KV, HEAD_DIM)
    qx = jnp.einsum("sthgd,ph->spdhgt", q5, jnp.eye(MOBA_KV_HEADS, dtype=q.dtype)).reshape(
        s, KV_DIM, LANES)
    pad = ((0, 0), (0, Q_BLOCK - dec_seq), (0, 0))
    k_pad, v_pad = jnp.pad(k_new, pad), jnp.pad(v_new, pad)
    kt = cache_k.transpose(0, 2, 3, 1).reshape(n_pool, KV_DIM, page)
    vt = cache_v.transpose(0, 2, 3, 1).reshape(n_pool, KV_DIM, page)

    def page_spec(i):
        return pl.BlockSpec((1, KV_DIM, page), lambda b, j, pt: (pt[b, j * gp + i], 0, 0))

    seq_spec = lambda r, w: pl.BlockSpec((1, r, w), lambda b, j, pt: (b, 0, 0))
    grid_spec = pltpu.PrefetchScalarGridSpec(
        num_scalar_prefetch=1,
        grid=(s, n_pages // gp),
        in_specs=([seq_spec(KV_DIM, LANES)] + [page_spec(i) for i in range(gp)] * 2
                  + [seq_spec(Q_BLOCK, KV_DIM), seq_spec(Q_BLOCK, KV_DIM), _vmem_full(), _vmem_full()]),
        out_specs=seq_spec(HEAD_DIM, LANES),
        scratch_shapes=[pltpu.VMEM((nbp, MOBA_BLOCK, LANES), F32),
                        pltpu.VMEM((nbp, KV_DIM, MOBA_BLOCK), BF16),
                        pltpu.VMEM((nbp, LANES), F32),
                        pltpu.VMEM((nbp, LANES), F32)],
    )
    out = pl.pallas_call(
        functools.partial(_moba_sample_kernel, pages_per_step=gp, n_far=n_far, dec_seq=dec_seq),
        grid_spec=grid_spec,
        out_shape=jax.ShapeDtypeStruct((s, HEAD_DIM, LANES), F32),
        compiler_params=_cparams("parallel", "arbitrary"),
        name="moba_sample",
    )(page_table, qx, *([kt] * gp), *([vt] * gp), k_pad, v_pad, bias_tiles, farb)
    out = out.reshape(s, HEAD_DIM, MOBA_KV_HEADS, Q_PER_KV, dec_seq).transpose(0, 4, 2, 3, 1)
    return out.reshape(s, dec_seq, D_MODEL).astype(BF16)


def _memkv_kernel(mem_ref, gn_ref, w_ref, gk_ref, e_ref, et_ref, k_ref, v_ref):
    x = mem_ref[...]
    h = (x * lax.rsqrt(jnp.mean(x * x, axis=-1, keepdims=True) + EPS) * gn_ref[...]).astype(BF16)
    kv = _dot(h, w_ref[...])
    k = kv[:, :D_MODEL]
    k_ref[...] = k * _group_rr(k, e_ref, et_ref, MEM_HEAD_DIM) * gk_ref[...]
    v_ref[...] = kv[:, D_MODEL:]


def _memkv(mem, wts):
    n = mem.shape[0]
    tm = min(n, 256)
    assert n % tm == 0
    consts = wts["memkv"]
    row = pl.BlockSpec((tm, D_MODEL), lambda i: (i, 0))
    return pl.pallas_call(
        _memkv_kernel,
        grid=(n // tm,),
        in_specs=[row] + [_vmem_full()] * len(consts),
        out_specs=(row, row),
        out_shape=(jax.ShapeDtypeStruct((n, D_MODEL), F32),) * 2,
        compiler_params=_cparams("parallel"),
        name="mem_kv",
    )(mem, *consts)


def _memattn_kernel(q_ref, k_ref, v_ref, o_ref):
    q = q_ref[0]
    for h in range(MEM_HEADS):
        sl = slice(h * MEM_HEAD_DIM, (h + 1) * MEM_HEAD_DIM)
        kh = k_ref[0, :, sl].astype(BF16)
        vh = v_ref[0, :, sl].astype(BF16)
        s = lax.dot_general(q[:, sl], kh, (((1,), (1,)), ((), ())), preferred_element_type=F32)
        p = jnp.exp(s - jnp.max(s, axis=-1, keepdims=True))
        l = jnp.sum(p, axis=-1, keepdims=True)
        o_ref[0, :, sl] = (_dot(p.astype(BF16), vh) / l).astype(BF16)


def _memattn(qm, mk, mv, tq):
    b, t, _ = qm.shape
    m = mk.shape[1]
    assert t % tq == 0
    qspec = pl.BlockSpec((1, tq, D_MODEL), lambda i, a: (i, a, 0))
    mspec = pl.BlockSpec((1, m, D_MODEL), lambda i, a: (i, 0, 0))
    return pl.pallas_call(
        _memattn_kernel,
        grid=(b, t // tq),
        in_specs=[qspec, mspec, mspec],
        out_specs=qspec,
        out_shape=jax.ShapeDtypeStruct((b, t, D_MODEL), BF16),
        compiler_params=_cparams("parallel", "arbitrary"),
        name="mem_attn",
    )(qm, mk, mv)


def _merge_kernel(x_ref, ys_ref, ym_ref, yc_ref, gate_ref, ws_ref, wm_ref, wc_ref, wo_ref, gf_ref,
                  x1_ref, hn_ref):
    g = gate_ref[...].astype(F32)
    merged = (g[:, :D_MODEL] * _dot(ys_ref[...], ws_ref[...])
              + g[:, D_MODEL:2 * D_MODEL] * _dot(ym_ref[...], wm_ref[...])
              + g[:, 2 * D_MODEL:] * _dot(yc_ref[...], wc_ref[...]))
    x1 = x_ref[...] + _dot(merged.astype(BF16), wo_ref[...])
    x1_ref[...] = x1
    hn_ref[...] = x1 * lax.rsqrt(jnp.mean(x1 * x1, axis=-1, keepdims=True) + EPS) * gf_ref[...]


def _merge(x, y_ssd, y_moba, y_mem, gate, wts, tm):
    n = x.shape[0]
    assert n % tm == 0
    consts = wts["merge"]
    row = lambda w: pl.BlockSpec((tm, w), lambda i: (i, 0))
    return pl.pallas_call(
        _merge_kernel,
        grid=(n // tm,),
        in_specs=[row(D_MODEL), row(D_INNER), row(D_MODEL), row(D_MODEL), row(N_BRANCHES * D_MODEL)]
                 + [_vmem_full()] * len(consts),
        out_specs=(row(D_MODEL), row(D_MODEL)),
        out_shape=(jax.ShapeDtypeStruct((n, D_MODEL), F32),) * 2,
        compiler_params=_cparams("parallel"),
        name="merge_out",
    )(x, y_ssd, y_moba, y_mem, gate, *consts)


def _topk_rows(vals, payload, k, v_ref, p_ref):
    r = vals.shape[0]
    rows = lax.broadcasted_iota(I32, vals.shape, 0).astype(F32)
    for i in range(k):
        m = jnp.max(vals, axis=0, keepdims=True)
        pos = jnp.min(jnp.where(vals == m, rows, float(r)), axis=0, keepdims=True)
        hit = rows == pos
        if payload is None:
            p_ref[i:i + 1, :] = pos
        else:
            p_ref[i:i + 1, :] = jnp.max(jnp.where(hit, payload, -1.0), axis=0, keepdims=True)
        v_ref[i:i + 1, :] = m
        vals = jnp.where(hit, NEG_INF, vals)


CAND_COUNTS = tuple(PEER_TOPK // (a + 1) for a in range(PEER_TOPK))
CAND_ROWS = -(-sum(CAND_COUNTS) // SUBLANES) * SUBLANES


def _peer_topk_kernel(hn_ref, wpq_ref, keys_ref, eid_ref, g_ref,
                      s1v_ref, s1i_ref, s2v_ref, s2i_ref, cand_ref, cid_ref, tv_ref, ti_ref):
    qv = _dot(hn_ref[...].astype(BF16), wpq_ref[...])
    K = PEER_TOPK
    n_cand = sum(CAND_COUNTS)
    tm = cand_ref.shape[1]
    cand_ref[n_cand:, :] = jnp.full((CAND_ROWS - n_cand, tm), NEG_INF, F32)
    cid_ref[n_cand:, :] = jnp.zeros((CAND_ROWS - n_cand, tm), F32)
    for h in range(PEER_HEADS):
        for x, (sv_ref, si_ref) in enumerate(((s1v_ref, s1i_ref), (s2v_ref, s2i_ref))):
            c0 = (h * 2 + x) * PEER_HALF
            q_hi, q_lo = _split_bf16(qv[:, c0:c0 + PEER_HALF])
            ql = jnp.concatenate([q_hi, q_lo], axis=1)
            st = lax.dot_general(keys_ref[h * 2 + x], ql, (((1,), (1,)), ((), ())),
                                 preferred_element_type=F32)
            _topk_rows(st, None, K, sv_ref, si_ref)
        s1, i1, s2, i2 = s1v_ref[...], s1i_ref[...], s2v_ref[...], s2i_ref[...]
        r0 = 0
        for a, cnt in enumerate(CAND_COUNTS):
            cand_ref[r0:r0 + cnt, :] = s1[a:a + 1, :] + s2[0:cnt, :]
            cid_ref[r0:r0 + cnt, :] = i1[a:a + 1, :] * float(PEER_KEYS) + i2[0:cnt, :]
            r0 += cnt
        _topk_rows(cand_ref[...], cid_ref[...], K, tv_ref, ti_ref)
        top = tv_ref[...]
        e = jnp.exp(top - top[0:1, :])
        g_ref[h * K:(h + 1) * K, :] = e / jnp.sum(e, axis=0, keepdims=True)
        eid_ref[h * K:(h + 1) * K, :] = ti_ref[...].astype(I32)


def _peer_topk(hn, wts, tm):
    n = hn.shape[0]
    assert n % tm == 0
    consts = wts["peer_topk"]
    rows = PEER_HEADS * PEER_TOPK
    col = pl.BlockSpec((rows, tm), lambda i: (0, i))
    K = PEER_TOPK
    return pl.pallas_call(
        _peer_topk_kernel,
        grid=(n // tm,),
        in_specs=[pl.BlockSpec((tm, D_MODEL), lambda i: (i, 0))] + [_vmem_full()] * len(consts),
        out_specs=(col, col),
        out_shape=(jax.ShapeDtypeStruct((rows, n), I32), jax.ShapeDtypeStruct((rows, n), F32)),
        scratch_shapes=[pltpu.VMEM((K, tm), F32)] * 4
                       + [pltpu.VMEM((CAND_ROWS, tm), F32)] * 2
                       + [pltpu.VMEM((K, tm), F32)] * 2,
        compiler_params=_cparams("parallel"),
        name="peer_topk",
    )(hn, *consts)


TABLE_ROWS = 4
HI_MASK = -65536


def _pack_table(tbl):
    e = tbl.shape[0]
    bits = lax.bitcast_convert_type(tbl.astype(BF16), jnp.uint16).astype(jnp.uint32)
    half = D_MODEL // 2
    word = bits[:, :half] | (bits[:, half:] << 16)
    return lax.bitcast_convert_type(word, I32).reshape(e, TABLE_ROWS, LANES)


def _unpack_row(x):
    lo = pltpu.bitcast(x << 16, F32)
    hi = pltpu.bitcast(x & HI_MASK, F32)
    return lo, hi


def _peer_u_kernel(eid_ref, h_ref, g_ref, u_ref, fold_ref, coef_ref, p_ref, act4_ref):
    tt = h_ref.shape[0]
    n_sel = eid_ref.shape[1]
    grp = p_ref.shape[1] // LANES
    row_tok = lax.broadcasted_iota(I32, (grp * LANES, tt), 0) // LANES
    col_i = lax.broadcasted_iota(I32, (grp * LANES, tt), 1)
    act4_ref[...] = jnp.zeros(act4_ref.shape, F32)

    def token_group(gi, carry):
        t0 = gi * grp
        for g in range(grp):
            hh = h_ref[t0 + g]
            h_lo, h_hi = hh[0:TABLE_ROWS], hh[TABLE_ROWS:]
            for j in range(n_sel):
                lo, hi = _unpack_row(u_ref[eid_ref[t0 + g, j]])
                p_ref[j * TABLE_ROWS:(j + 1) * TABLE_ROWS, g * LANES:(g + 1) * LANES] = (
                    lo * h_lo + hi * h_hi)
        onehot = jnp.where(col_i == t0 + row_tok, 1.0, 0.0).astype(BF16)
        act4_ref[...] += _dot(p_ref[...].astype(BF16), onehot)
        return carry

    lax.fori_loop(0, tt // grp, token_group, 0)
    a_hi, a_lo = _split_bf16(act4_ref[...])
    act = _dot(fold_ref[...], a_hi) + _dot(fold_ref[...], a_lo)
    gelu = 0.5 * act * (1.0 + lax.erf(act * (1.0 / math.sqrt(2.0))))
    coef_ref[...] = g_ref[...] * gelu


N_ACC = 2


def _peer_v_kernel(eid_ref, coef_ref, x1_ref, v_ref, o_ref, crep_ref):
    tt = x1_ref.shape[0]
    n_sel = eid_ref.shape[1]
    lane = lax.broadcasted_iota(I32, (n_sel, tt), 1)

    def spread(t, buf):
        c = jnp.sum(jnp.where(lane == t, coef_ref[...], 0.0), axis=-1, keepdims=True)
        crep_ref[buf] = jnp.broadcast_to(c, (n_sel, LANES))

    def weighted_rows(t, buf):
        acc_lo = [jnp.zeros((TABLE_ROWS, LANES), F32) for _ in range(N_ACC)]
        acc_hi = [jnp.zeros((TABLE_ROWS, LANES), F32) for _ in range(N_ACC)]
        for j in range(n_sel):
            lo, hi = _unpack_row(v_ref[eid_ref[t, j]])
            cj = crep_ref[buf, j:j + 1, :]
            acc_lo[j % N_ACC] = acc_lo[j % N_ACC] + cj * lo
            acc_hi[j % N_ACC] = acc_hi[j % N_ACC] + cj * hi
        ff = jnp.concatenate([sum(acc_lo[1:], acc_lo[0]), sum(acc_hi[1:], acc_hi[0])], axis=0)
        o_ref[t] = x1_ref[t] + ff

    n_buf = crep_ref.shape[0]

    def token_group(i, carry):
        t0 = n_buf * i
        for u in range(1, n_buf):
            spread(t0 + u, u)
        weighted_rows(t0, 0)
        spread(jnp.minimum(t0 + n_buf, tt - 1), 0)
        for u in range(1, n_buf):
            weighted_rows(t0 + u, u)
        return carry

    spread(0, 0)
    lax.fori_loop(0, tt // n_buf, token_group, 0)


def _peer(x1, hn, wts, tm_topk, tt):
    n = x1.shape[0]
    grp = 8 if tt % 8 == 0 else 1
    assert n % tt == 0 and tt % 2 == 0
    n_sel = PEER_HEADS * PEER_TOPK
    eid_t, g_t = _peer_topk(hn, wts, tm_topk)
    eid = eid_t.T
    fold = jnp.asarray(np.repeat(np.eye(n_sel, dtype=np.float32), TABLE_ROWS, axis=1), BF16)
    tok3 = pl.BlockSpec((tt, SUBLANES, LANES), lambda i: (i, 0, 0))
    col = pl.BlockSpec((n_sel, tt), lambda i: (0, i))
    ids = pl.BlockSpec((tt, n_sel), lambda i: (i, 0), memory_space=pltpu.SMEM)
    coef = pl.pallas_call(
        _peer_u_kernel,
        grid=(n // tt,),
        in_specs=[ids, tok3, col, _vmem_full(), _vmem_full()],
        out_specs=col,
        out_shape=jax.ShapeDtypeStruct((n_sel, n), F32),
        scratch_shapes=[pltpu.VMEM((n_sel * TABLE_ROWS, grp * LANES), F32),
                        pltpu.VMEM((n_sel * TABLE_ROWS, tt), F32)],
        compiler_params=_cparams("parallel"),
        name="peer_u",
    )(eid, hn.reshape(n, SUBLANES, LANES), g_t, wts["peer_u"], fold)
    out = pl.pallas_call(
        _peer_v_kernel,
        grid=(n // tt,),
        in_specs=[ids, col, tok3, _vmem_full()],
        out_specs=tok3,
        out_shape=jax.ShapeDtypeStruct((n, SUBLANES, LANES), F32),
        scratch_shapes=[pltpu.VMEM((2, n_sel, LANES), F32)],
        compiler_params=_cparams("parallel"),
        name="peer_v",
    )(eid, coef, x1.reshape(n, SUBLANES, LANES), wts["peer_v"])
    return out.reshape(n, D_MODEL)


IN_WIDTHS = (D_INNER, CONV_DIM, SSM_HEADS, D_MODEL, KV_DIM, KV_DIM, D_MODEL, N_BRANCHES * D_MODEL)


def _row(v):
    return v.reshape(1, -1).astype(F32)


def _prepare(p):
    offs = np.cumsum((0,) + IN_WIDTHS)
    seg = [p["w_in"][:, offs[i]:offs[i + 1]].astype(BF16) for i in range(len(IN_WIDTHS))]
    wz, wxbc, wdt, wq, wk, wv, wqm, wg = seg
    wdt = jnp.pad(wdt, ((0, 0), (0, LANES - SSM_HEADS)))
    e64, et64 = _group_indicators(D_MODEL, HEAD_DIM)
    e64k, et64k = _group_indicators(KV_DIM, HEAD_DIM)
    e256, et256 = _group_indicators(D_MODEL, MEM_HEAD_DIM)
    gq = _row(jnp.tile(p["g_q"], MOBA_HEADS)) * (HEAD_DIM ** -0.5 * LOG2E)
    gk = _row(jnp.tile(p["g_k"], MOBA_KV_HEADS))
    gmq = _row(jnp.tile(p["g_mq"], MEM_HEADS)) * MEM_HEAD_DIM ** -0.5
    inproj = [_row(p["g_mix_norm"]), wz, wxbc, wdt, wq, wk, wv, wqm, wg, _row(p["b_gate"]),
              gq, gk, gmq, e64, et64, e64k, et64k, e256, et256]

    pad_h = lambda v: jnp.pad(_row(v), ((0, 0), (0, LANES - SSM_HEADS)))
    e2 = np.zeros((LANES, D_INNER), np.float32)
    e2[np.arange(D_INNER) // SSM_HEAD_DIM, np.arange(D_INNER)] = 1.0
    e2 = jnp.asarray(np.concatenate([e2, e2], axis=0), BF16)
    eg, etg2 = _group_indicators(D_INNER, D_INNER // SSM_GROUPS)
    ssd = [p["conv_w"].astype(F32), _row(p["conv_b"]), pad_h(p["dt_bias"]), pad_h(p["a_log"]),
           _row(jnp.repeat(p["d_skip"], SSM_HEAD_DIM)), _row(p["g_ssm_norm"]), e2, eg, etg2]

    memkv = [_row(p["g_mem_norm"]), p["w_mem_kv"].astype(BF16),
             _row(jnp.tile(p["g_mk"], MEM_HEADS)), e256, et256]
    merge = [p["w_ssm_out"].astype(BF16), p["w_moba_out"].astype(BF16), p["w_mem_out"].astype(BF16),
             p["w_out"].astype(BF16), _row(p["g_ffn_norm"])]
    keys = p["peer_keys"].astype(BF16).reshape(PEER_HEADS * 2, PEER_KEYS, PEER_HALF)
    peer_topk = [p["w_pq"].astype(BF16), jnp.concatenate([keys, keys], axis=-1)]
    return {"inproj": inproj, "ssd": ssd, "memkv": memkv, "merge": merge, "peer_topk": peer_topk,
            "peer_u": _pack_table(p["peer_u"]), "peer_v": _pack_table(p["peer_v"])}


def _tile(n, pref):
    return pref if n % pref == 0 else n


def _group_forward(x, wts, ssd_state, attn_fn, mem_k, mem_v):
    b, t, _ = x.shape
    n = b * t
    x2 = x.reshape(n, D_MODEL)
    z, xbc, dt, q, k, v, qm, gate = _inproj(x2, wts, _tile(n, 256))
    conv0, ssm0, q_rows, n_valid = ssd_state
    y_ssd, ssm_new, conv_new = _ssd(xbc.reshape(b, t, CONV_DIM), z.reshape(b, t, D_INNER),
                                    dt.reshape(b, t, LANES), conv0, ssm0, wts, q_rows, n_valid)
    k3, v3 = k.reshape(b, t, KV_DIM), v.reshape(b, t, KV_DIM)
    y_moba = attn_fn(q.reshape(b, t, D_MODEL), k3, v3)
    y_mem = _memattn(qm.reshape(b, t, D_MODEL), mem_k, mem_v, _tile(t, 512))
    x1, hn = _merge(x2, y_ssd.reshape(n, D_INNER), y_moba.reshape(n, D_MODEL),
                    y_mem.reshape(n, D_MODEL), gate, wts, _tile(n, 256))
    out = _peer(x1, hn, wts, _tile(n, 256), _tile(n, 128))
    return (out.reshape(b, t, D_MODEL),
            ssm_new.reshape(b, SSM_HEADS, SSM_HEAD_DIM, D_STATE), conv_new,
            k3.reshape(b, t, MOBA_KV_HEADS, HEAD_DIM), v3.reshape(b, t, MOBA_KV_HEADS, HEAD_DIM))


def kernel(x_prompt, x_sample, cache_k, cache_v, state_ssm, state_conv, cache_mem_k, cache_mem_v,
           page_table, mem_prompt, g_mix_norm, w_in, conv_w, conv_b, dt_bias, a_log, d_skip,
           g_ssm_norm, w_ssm_out, g_q, g_k, rel_bias, w_moba_out, g_mem_norm, w_mem_kv, g_mq, g_mk,
           w_mem_out, b_gate, w_out, g_ffn_norm, w_pq, peer_keys, peer_u, peer_v):
    depth = w_in.shape[0]
    bp, tp, _ = x_prompt.shape
    sb, dec_seq, _ = x_sample.shape
    mem_len = mem_prompt.shape[1]
    n_pool, page = cache_k.shape[1], cache_k.shape[2]
    past_len = page_table.shape[1] * page
    assert past_len % MOBA_BLOCK == 0 and tp % MOBA_BLOCK == 0

    far_d = _far_distance()
    n_near_p = -(-(far_d + MOBA_BLOCK - 1) // MOBA_BLOCK) - 1
    bias_p = _bias_prompt(rel_bias, 2 * n_near_p + 2)
    farb_p = _row(jnp.repeat(rel_bias[NUM_BUCKETS - 1], Q_BLOCK)) * LOG2E
    nbp = past_len // MOBA_BLOCK
    n_far_s = min(max((past_len - (MOBA_BLOCK - 1) - far_d) // MOBA_BLOCK + 1, 0), nbp)
    bias_s = _bias_sample(rel_bias, past_len, n_far_s, nbp - n_far_s, dec_seq)
    farb_s = _row(jnp.repeat(rel_bias[NUM_BUCKETS - 1], dec_seq)) * LOG2E

    per_layer = dict(g_mix_norm=g_mix_norm, w_in=w_in, conv_w=conv_w, conv_b=conv_b, dt_bias=dt_bias,
                     a_log=a_log, d_skip=d_skip, g_ssm_norm=g_ssm_norm, w_ssm_out=w_ssm_out, g_q=g_q,
                     g_k=g_k, w_moba_out=w_moba_out, g_mem_norm=g_mem_norm, w_mem_kv=w_mem_kv,
                     g_mq=g_mq, g_mk=g_mk, w_mem_out=w_mem_out, b_gate=b_gate, w_out=w_out,
                     g_ffn_norm=g_ffn_norm, w_pq=w_pq, peer_keys=peer_keys, peer_u=peer_u,
                     peer_v=peer_v)
    xp, xs = x_prompt, x_sample
    outs = [[] for _ in range(10)]
    chunk = math.gcd(tp, SSD_CHUNK)
    for l in range(depth):
        wts = _prepare({name: val[l] for name, val in per_layer.items()})
        mk, mv = _memkv(mem_prompt.reshape(bp * mem_len, D_MODEL), wts)
        mk, mv = mk.reshape(bp, mem_len, D_MODEL), mv.reshape(bp, mem_len, D_MODEL)
        prompt_attn = functools.partial(_moba_prompt, bias_tiles=bias_p, farb=farb_p, n_near=n_near_p)
        xp, h_p, c_p, k_p, v_p = _group_forward(xp, wts, (None, None, chunk, chunk), prompt_attn, mk, mv)
        sample_attn = functools.partial(
            _moba_sample, cache_k=cache_k[l], cache_v=cache_v[l], page_table=page_table,
            bias_tiles=bias_s, farb=farb_s, n_far=n_far_s, pages_per_step=8)
        conv0 = jnp.pad(state_conv[l], ((0, 0), (SUBLANES - (CONV_WIDTH - 1), 0), (0, 0)))
        ssm0 = state_ssm[l].reshape(sb, D_INNER, D_STATE)
        xs, h_s, c_s, k_s, v_s = _group_forward(
            xs, wts, (conv0, ssm0, Q_BLOCK, dec_seq), sample_attn,
            cache_mem_k[l].reshape(sb, mem_len, D_MODEL), cache_mem_v[l].reshape(sb, mem_len, D_MODEL))
        heads = (MEM_HEADS, MEM_HEAD_DIM)
        for lst, val in zip(outs, (k_p, v_p, k_s, v_s, h_p, h_s, c_p, c_s,
                                   mk.reshape(bp, mem_len, *heads), mv.reshape(bp, mem_len, *heads))):
            lst.append(val)
    return (xp, xs) + tuple(jnp.stack(lst) for lst in outs)
```

```python
import functools
import math

import numpy as np
import jax
import jax.numpy as jnp
from jax import lax
from jax.experimental import pallas as pl
from jax.experimental.pallas import tpu as pltpu

F32 = jnp.float32
BF16 = jnp.bfloat16
I32 = jnp.int32
NEG_INF = float("-inf")

D_MODEL = 1024
D_INNER = 2048
SSM_HEAD_DIM = 64
SSM_HEADS = 32
SSM_GROUPS = 4
D_STATE = 128
CONV_WIDTH = 4
CONV_DIM = D_INNER + 2 * SSM_GROUPS * D_STATE
SSD_CHUNK = 256
HEAD_DIM = 64
MOBA_HEADS = 16
MOBA_KV_HEADS = 4
Q_PER_KV = 4
KV_DIM = MOBA_KV_HEADS * HEAD_DIM
MOBA_BLOCK = 256
MOBA_TOPK = 3
Q_BLOCK = 128
NUM_BUCKETS = 32
MAX_DISTANCE = 2048
MEM_HEADS = 4
MEM_HEAD_DIM = 256
PEER_HEADS = 8
PEER_KEYS = 128
PEER_HALF = 128
PEER_TOPK = 16
N_BRANCHES = 3
EPS = 1e-6
LOG2E = math.log2(math.e)

LANES = 128
SUBLANES = 8
VMEM_LIMIT_BYTES = 56 * 1024 * 1024


def _cparams(*sem):
    return pltpu.CompilerParams(dimension_semantics=sem, vmem_limit_bytes=VMEM_LIMIT_BYTES)


def _vmem_full():
    return pl.BlockSpec(memory_space=pltpu.VMEM)


def _dot(a, b):
    return jnp.dot(a, b, preferred_element_type=F32)


def _split_bf16(x):
    hi = x.astype(BF16)
    lo = (x - hi.astype(F32)).astype(BF16)
    return hi, lo


def _group_indicators(dim, gsize):
    e = np.zeros((dim, LANES), np.float32)
    e[np.arange(dim), np.arange(dim) // gsize] = 1.0
    et2 = np.concatenate([e.T, e.T], axis=0)
    return jnp.asarray(e, BF16), jnp.asarray(et2, BF16)


def _group_rr(x, e_ref, et2_ref, gsize):
    ssq = _dot((x * x).astype(BF16), e_ref[...])
    r = lax.rsqrt(ssq * (1.0 / gsize) + EPS)
    r_hi, r_lo = _split_bf16(r)
    return _dot(jnp.concatenate([r_hi, r_lo], axis=1), et2_ref[...])


def _inproj_kernel(x_ref, gmix_ref, wz_ref, wxbc_ref, wdt_ref, wq_ref, wk_ref, wv_ref, wqm_ref,
                   wg_ref, bgate_ref, gq_ref, gk_ref, gmq_ref, e64_ref, et64_ref, e64k_ref,
                   et64k_ref, e256_ref, et256_ref,
                   z_ref, xbc_ref, dt_ref, q_ref, k_ref, v_ref, qm_ref, gate_ref):
    x = x_ref[...]
    h = (x * lax.rsqrt(jnp.mean(x * x, axis=-1, keepdims=True) + EPS) * gmix_ref[...]).astype(BF16)
    z_ref[...] = _dot(h, wz_ref[...]).astype(BF16)
    xbc_ref[...] = _dot(h, wxbc_ref[...])
    dt_ref[...] = _dot(h, wdt_ref[...])
    q = _dot(h, wq_ref[...])
    q_ref[...] = (q * _group_rr(q, e64_ref, et64_ref, HEAD_DIM) * gq_ref[...]).astype(BF16)
    k = _dot(h, wk_ref[...])
    k_ref[...] = k * _group_rr(k, e64k_ref, et64k_ref, HEAD_DIM) * gk_ref[...]
    v_ref[...] = _dot(h, wv_ref[...])
    qm = _dot(h, wqm_ref[...])
    qm_ref[...] = (qm * _group_rr(qm, e256_ref, et256_ref, MEM_HEAD_DIM) * gmq_ref[...]).astype(BF16)
    gate_ref[...] = jax.nn.sigmoid(_dot(h, wg_ref[...]) + bgate_ref[...]).astype(BF16)


def _inproj(x, wts, tm):
    n = x.shape[0]
    assert n % tm == 0
    consts = wts["inproj"]
    row = lambda w: pl.BlockSpec((tm, w), lambda i: (i, 0))
    out_shape = (
        jax.ShapeDtypeStruct((n, D_INNER), BF16),
        jax.ShapeDtypeStruct((n, CONV_DIM), F32),
        jax.ShapeDtypeStruct((n, LANES), F32),
        jax.ShapeDtypeStruct((n, D_MODEL), BF16),
        jax.ShapeDtypeStruct((n, KV_DIM), F32),
        jax.ShapeDtypeStruct((n, KV_DIM), F32),
        jax.ShapeDtypeStruct((n, D_MODEL), BF16),
        jax.ShapeDtypeStruct((n, N_BRANCHES * D_MODEL), BF16),
    )
    return pl.pallas_call(
        _inproj_kernel,
        grid=(n // tm,),
        in_specs=[row(D_MODEL)] + [_vmem_full()] * len(consts),
        out_specs=tuple(row(s.shape[1]) for s in out_shape),
        out_shape=out_shape,
        compiler_params=_cparams("parallel"),
        name="inproj",
    )(x, *consts)


def _ssd_kernel(*refs, q_rows, n_valid, has_init):
    if has_init:
        (xbc_ref, z_ref, dt_ref, conv0_ref, ssm0_ref, cw_ref, cb_ref, dtb_ref, alog_ref, dskip_ref,
         gssm_ref, e2_ref, eg_ref, etg2_ref, y_ref, ssm_out_ref, conv_out_ref,
         xw_ref, st_ref, yb_ref) = refs
    else:
        (xbc_ref, z_ref, dt_ref, cw_ref, cb_ref, dtb_ref, alog_ref, dskip_ref,
         gssm_ref, e2_ref, eg_ref, etg2_ref, y_ref, ssm_out_ref, conv_out_ref,
         xw_ref, st_ref, yb_ref) = refs
    c = pl.program_id(1)
    last = pl.num_programs(1) - 1
    Q, NV = q_rows, n_valid
    G = SSM_GROUPS
    GW = D_INNER // G

    @pl.when(c == 0)
    def _init():
        if has_init:
            xw_ref[0:SUBLANES, :] = conv0_ref[0]
            st_ref[...] = ssm0_ref[0].T
        else:
            xw_ref[0:SUBLANES, :] = jnp.zeros((SUBLANES, CONV_DIM), F32)
            st_ref[...] = jnp.zeros_like(st_ref)

    xw_ref[SUBLANES:SUBLANES + NV, :] = xbc_ref[0]
    if NV < Q:
        xw_ref[SUBLANES + NV:SUBLANES + Q, :] = jnp.zeros((Q - NV, CONV_DIM), F32)

    acc = jnp.broadcast_to(cb_ref[...], (Q, CONV_DIM))
    for kk in range(CONV_WIDTH):
        off = SUBLANES - (CONV_WIDTH - 1) + kk
        acc = acc + cw_ref[kk:kk + 1, :] * xw_ref[off:off + Q, :]
    xc = jax.nn.silu(acc)
    xs = xc[:, :D_INNER]
    bm = xc[:, D_INNER:D_INNER + G * D_STATE]
    cm = xc[:, D_INNER + G * D_STATE:]

    dtv = dt_ref[0]
    if NV < Q:
        dtv = jnp.concatenate([dtv, jnp.zeros((Q - NV, LANES), F32)], axis=0)
    xdt_pre = dtv + dtb_ref[...]
    dt = jnp.maximum(xdt_pre, 0.0) + jnp.log1p(jnp.exp(-jnp.abs(xdt_pre)))
    row_i = lax.broadcasted_iota(I32, (Q, LANES), 0)
    if NV < Q:
        dt = jnp.where(row_i < NV, dt, 0.0)
    a = -jnp.exp(alog_ref[...])
    da = dt * a

    ri = lax.broadcasted_iota(I32, (Q, Q), 0)
    ci = lax.broadcasted_iota(I32, (Q, Q), 1)
    causal = ri >= ci
    lower = causal.astype(F32)
    upper = (ri <= ci).astype(F32)
    cum = jnp.dot(lower, da, preferred_element_type=F32, precision=lax.Precision.HIGHEST)
    cum_t = jnp.dot(da.T, upper, preferred_element_type=F32, precision=lax.Precision.HIGHEST)

    def expand(v):
        hi, lo = _split_bf16(v)
        return _dot(jnp.concatenate([hi, lo], axis=1), e2_ref[...])

    dt_x = expand(dt)
    ecum_x = expand(jnp.exp(cum))
    toend_x = expand(jnp.exp(cum[Q - 1:Q, :] - cum))
    xdt = xs * dt_x
    xdt_b = xdt.astype(BF16)
    xw_b = (xdt * toend_x).astype(BF16)
    dec_row = ecum_x[Q - 1:Q, :]
    lane_lo = lax.broadcasted_iota(I32, (Q, LANES), 1) < SSM_HEAD_DIM

    for g in range(G):
        cg = cm[:, g * D_STATE:(g + 1) * D_STATE].astype(BF16)
        bg = bm[:, g * D_STATE:(g + 1) * D_STATE]
        bg_b = bg.astype(BF16)
        cb = lax.dot_general(cg, bg_b, (((1,), (1,)), ((), ())), preferred_element_type=F32)
        st_g = st_ref[:, g * GW:(g + 1) * GW]
        y_off = _dot(cg, st_g.astype(BF16))
        for pair in range(GW // LANES):
            col0 = g * GW + pair * LANES
            x2 = xdt_b[:, col0:col0 + LANES]
            halves = []
            for e in range(2):
                hh = col0 // SSM_HEAD_DIM + e
                seg = cum[:, hh:hh + 1] - cum_t[hh:hh + 1, :]
                lmat = jnp.exp(jnp.where(causal, seg, NEG_INF))
                halves.append(_dot((cb * lmat).astype(BF16), x2))
            yb_ref[:, col0:col0 + LANES] = jnp.where(lane_lo, halves[0], halves[1])
        yb_ref[:, g * GW:(g + 1) * GW] = (yb_ref[:, g * GW:(g + 1) * GW]
                                          + y_off * ecum_x[:, g * GW:(g + 1) * GW])
        st_ref[:, g * GW:(g + 1) * GW] = (st_g * dec_row[:, g * GW:(g + 1) * GW]
                                          + _dot(bg.T.astype(BF16), xw_b[:, g * GW:(g + 1) * GW]))

    y = yb_ref[...] + xs * dskip_ref[...]
    zf = z_ref[0].astype(F32)
    if NV < Q:
        zf = jnp.concatenate([zf, jnp.zeros((Q - NV, D_INNER), F32)], axis=0)
    yz = y * jax.nn.silu(zf)
    yn = yz * _group_rr(yz, eg_ref, etg2_ref, GW) * gssm_ref[...]
    y_ref[0] = yn[0:NV, :].astype(BF16)

    xw_ref[0:SUBLANES, :] = xw_ref[NV:NV + SUBLANES, :]

    @pl.when(c == last)
    def _fin():
        ssm_out_ref[0] = st_ref[...].T
        conv_out_ref[0] = xw_ref[SUBLANES - (CONV_WIDTH - 1):SUBLANES, :]


def _ssd(xbc, z, dt, conv0, ssm0, wts, q_rows, n_valid):
    b, t, _ = xbc.shape
    assert t % n_valid == 0 and n_valid >= CONV_WIDTH - 1
    nc = t // n_valid
    has_init = conv0 is not None
    consts = wts["ssd"]
    tile = lambda w: pl.BlockSpec((1, n_valid, w), lambda i, c: (i, c, 0))
    in_specs = [tile(CONV_DIM), tile(D_INNER), tile(LANES)]
    args = [xbc, z, dt]
    if has_init:
        in_specs += [pl.BlockSpec((1, SUBLANES, CONV_DIM), lambda i, c: (i, 0, 0)),
                     pl.BlockSpec((1, D_INNER, D_STATE), lambda i, c: (i, 0, 0))]
        args += [conv0, ssm0]
    in_specs += [_vmem_full()] * len(consts)
    out_shape = (jax.ShapeDtypeStruct((b, t, D_INNER), BF16),
                 jax.ShapeDtypeStruct((b, D_INNER, D_STATE), F32),
                 jax.ShapeDtypeStruct((b, CONV_WIDTH - 1, CONV_DIM), F32))
    out_specs = (tile(D_INNER),
                 pl.BlockSpec((1, D_INNER, D_STATE), lambda i, c: (i, 0, 0)),
                 pl.BlockSpec((1, CONV_WIDTH - 1, CONV_DIM), lambda i, c: (i, 0, 0)))
    return pl.pallas_call(
        functools.partial(_ssd_kernel, q_rows=q_rows, n_valid=n_valid, has_init=has_init),
        grid=(b, nc),
        in_specs=in_specs,
        out_specs=out_specs,
        out_shape=out_shape,
        scratch_shapes=[pltpu.VMEM((SUBLANES + q_rows, CONV_DIM), F32),
                        pltpu.VMEM((D_STATE, D_INNER), F32),
                        pltpu.VMEM((q_rows, D_INNER), F32)],
        compiler_params=_cparams("parallel", "arbitrary"),
        name="ssd_scan",
    )(*args, *consts)


def _rel_bucket(dist):
    n = jnp.maximum(dist, 0)
    max_exact = NUM_BUCKETS // 2
    nf = jnp.maximum(n, 1).astype(F32)
    large = max_exact + (jnp.log(nf / max_exact) / math.log(MAX_DISTANCE / max_exact)
                         * (NUM_BUCKETS - max_exact)).astype(I32)
    large = jnp.minimum(large, NUM_BUCKETS - 1)
    return jnp.where(n < max_exact, n, large)


def _far_distance():
    d = np.arange(1, 4 * MAX_DISTANCE, dtype=np.float64)
    large = 16 + np.floor(np.log(d / 16) / math.log(MAX_DISTANCE / 16) * 16)
    below = np.nonzero(large < NUM_BUCKETS - 1)[0]
    return int(d[below[-1]]) + 1 + 1


def _bias_from_dist(dist, rbl_ref):
    bucket = _rel_bucket(dist)
    val = jnp.zeros(dist.shape, F32)
    for b in range(NUM_BUCKETS):
        val = jnp.where(bucket == b, rbl_ref[b:b + 1, :], val)
    return jnp.where(dist >= 0, val * LOG2E, NEG_INF)


def _bias_prompt_kernel(rbl_ref, o_ref):
    di = pl.program_id(0)
    w = Q_PER_KV * Q_BLOCK
    j = lax.broadcasted_iota(I32, (MOBA_BLOCK, w), 0)
    i = lax.broadcasted_iota(I32, (MOBA_BLOCK, w), 1) % Q_BLOCK
    o_ref[0, 0] = _bias_from_dist(di * Q_BLOCK + i - j, rbl_ref.at[0]).astype(BF16)


def _bias_prompt(rel_bias, n_tiles):
    w = Q_PER_KV * Q_BLOCK
    rbl = jnp.repeat(rel_bias.reshape(NUM_BUCKETS, MOBA_KV_HEADS, Q_PER_KV).transpose(1, 0, 2),
                     Q_BLOCK, axis=2)
    return pl.pallas_call(
        _bias_prompt_kernel,
        grid=(n_tiles, MOBA_KV_HEADS),
        in_specs=[pl.BlockSpec((1, NUM_BUCKETS, w), lambda d, h: (h, 0, 0))],
        out_specs=pl.BlockSpec((1, 1, MOBA_BLOCK, w), lambda d, h: (d, h, 0, 0)),
        out_shape=jax.ShapeDtypeStruct((n_tiles, MOBA_KV_HEADS, MOBA_BLOCK, w), BF16),
        compiler_params=_cparams("parallel", "parallel"),
        name="moba_bias_prompt",
    )(rbl)


def _bias_sample_kernel(rbl_ref, o_ref, *, past_len, first_block, n_near, dec_seq):
    r = pl.program_id(0)
    kpos0 = jnp.where(r < n_near, (first_block + r) * MOBA_BLOCK, past_len)
    j = lax.broadcasted_iota(I32, (MOBA_BLOCK, LANES), 0)
    t = lax.broadcasted_iota(I32, (MOBA_BLOCK, LANES), 1) % dec_seq
    o_ref[0] = _bias_from_dist(past_len + t - (kpos0 + j), rbl_ref)


def _bias_sample(rel_bias, past_len, first_block, n_near, dec_seq):
    rbl = jnp.repeat(rel_bias, dec_seq, axis=1)
    return pl.pallas_call(
        functools.partial(_bias_sample_kernel, past_len=past_len, first_block=first_block,
                          n_near=n_near, dec_seq=dec_seq),
        grid=(n_near + 1,),
        in_specs=[_vmem_full()],
        out_specs=pl.BlockSpec((1, MOBA_BLOCK, LANES), lambda r: (r, 0, 0)),
        out_shape=jax.ShapeDtypeStruct((n_near + 1, MOBA_BLOCK, LANES), F32),
        compiler_params=_cparams("parallel"),
        name="moba_bias_sample",
    )(rbl)


def _kmean_kernel(k_ref, o_ref, *, nb):
    for b in range(nb):
        blk = k_ref[0, b * MOBA_BLOCK:(b + 1) * MOBA_BLOCK, :]
        o_ref[0, b:b + 1, :] = jnp.sum(blk, axis=0, keepdims=True) * (1.0 / MOBA_BLOCK)


def _kmean(k):
    b, t, _ = k.shape
    nb = t // MOBA_BLOCK
    return pl.pallas_call(
        functools.partial(_kmean_kernel, nb=nb),
        grid=(b,),
        in_specs=[pl.BlockSpec((1, t, KV_DIM), lambda i: (i, 0, 0))],
        out_specs=pl.BlockSpec((1, nb, KV_DIM), lambda i: (i, 0, 0)),
        out_shape=jax.ShapeDtypeStruct((b, nb, KV_DIM), F32),
        compiler_params=_cparams("parallel"),
        name="moba_kmean",
    )(k)


def _select_blocks(gate, own, n_sel_rows):
    nb = gate.shape[0]
    blk = lax.broadcasted_iota(I32, gate.shape, 0)
    sel = jnp.zeros(gate.shape, F32)
    for t in range(MOBA_TOPK):
        m = jnp.max(gate, axis=0, keepdims=True)
        pos = jnp.min(jnp.where(gate == m, blk, nb), axis=0, keepdims=True)
        hit = blk == pos
        sel = jnp.where(hit, jnp.maximum(sel, jnp.where(t < n_sel_rows, 1.0, 0.0)), sel)
        gate = jnp.where(hit, NEG_INF, gate)
    return sel, blk


V_ROWS = HEAD_DIM + 16


def _moba_prompt_kernel(qt_ref, k_ref, vt_ref, km_ref, bias_ref, farb_ref, o_ref,
                        qx_ref, add_ref, m_ref, acc_ref, s_ref, *, n_near):
    a = pl.program_id(1)
    own = a // 2
    par = a % 2
    nb = k_ref.shape[1]
    hw = Q_PER_KV * Q_BLOCK
    w = MOBA_KV_HEADS * hw

    row_h = lax.broadcasted_iota(I32, (KV_DIM, Q_BLOCK), 0) // HEAD_DIM
    for h in range(MOBA_KV_HEADS):
        for g in range(Q_PER_KV):
            piece = jnp.where(row_h == h, qt_ref[0, g], jnp.zeros((), BF16))
            col = (h * Q_PER_KV + g) * Q_BLOCK
            qx_ref[:, col:col + Q_BLOCK] = piece

    km_hi, km_lo = _split_bf16(km_ref[0])
    gate = _dot(km_hi, qx_ref[...]) + _dot(km_lo, qx_ref[...])
    blk0 = lax.broadcasted_iota(I32, (nb, w), 0)
    gate = jnp.where(blk0 < own, gate, NEG_INF)
    sel, blk = _select_blocks(gate, own, own)
    far = (own - blk) > n_near
    add = jnp.where(sel > 0.0, jnp.where(far, farb_ref[...], 0.0), NEG_INF)
    add_ref[...] = jnp.where(blk == own, 0.0, add)

    m_ref[...] = jnp.full(m_ref.shape, NEG_INF, F32)
    acc_ref[...] = jnp.zeros(acc_ref.shape, F32)

    def scores(b, s_out):
        kb = k_ref[0, b]
        for h in range(MOBA_KV_HEADS):
            s_out[h] = _dot(kb, qx_ref[:, h * hw:(h + 1) * hw])

    def attend(b, s_in, di):
        for h in range(MOBA_KV_HEADS):
            s = s_in[h]
            if di is not None:
                s = s + bias_ref[di, h].astype(F32)
            add = add_ref[pl.ds(b, 1), h * hw:(h + 1) * hw]
            m_prev = m_ref[h]
            m_new = jnp.maximum(m_prev, jnp.max(s, axis=0, keepdims=True) + add)
            p = jnp.exp2(s - (m_new - add))
            alpha = jnp.exp2(m_prev - m_new)
            vth = vt_ref[0, b, h * V_ROWS:(h + 1) * V_ROWS, :]
            acc_ref[h] = alpha * acc_ref[h] + _dot(vth, p.astype(BF16))
            m_ref[h] = m_new

    n_biased = jnp.minimum(n_near, own) + 1

    def near_body(idx, carry):
        scores(own - idx, s_ref)
        attend(own - idx, s_ref, par + 2 * idx)
        return carry

    lax.fori_loop(0, n_biased, near_body, 0)

    def far_body(idx, carry):
        scores(own - idx, s_ref)
        attend(own - idx, s_ref, None)
        return carry

    lax.fori_loop(n_biased, own + 1, far_body, 0)

    for h in range(MOBA_KV_HEADS):
        acc = acc_ref[h]
        o_ref[0, h] = (acc[0:HEAD_DIM, :] / acc[HEAD_DIM:HEAD_DIM + 1, :]).astype(BF16)


def _moba_prompt(q, k, v, bias_tiles, farb, n_near):
    b, t, _ = q.shape
    assert t % MOBA_BLOCK == 0
    nb, nq = t // MOBA_BLOCK, t // Q_BLOCK
    hw = Q_PER_KV * Q_BLOCK
    w = MOBA_KV_HEADS * hw
    km = _kmean(k)
    qt = q.reshape(b, t, MOBA_KV_HEADS, Q_PER_KV, HEAD_DIM).transpose(0, 3, 2, 4, 1).reshape(
        b, Q_PER_KV, KV_DIM, t)
    kb = k.astype(BF16).reshape(b, nb, MOBA_BLOCK, KV_DIM)
    vt = v.astype(BF16).reshape(b, nb, MOBA_BLOCK, MOBA_KV_HEADS, HEAD_DIM).transpose(0, 1, 3, 4, 2)
    ones_rows = jnp.zeros((V_ROWS - HEAD_DIM, MOBA_BLOCK), BF16).at[0].set(1.0)
    vt = jnp.concatenate(
        [vt, jnp.broadcast_to(ones_rows, (b, nb, MOBA_KV_HEADS) + ones_rows.shape)], axis=3)
    vt = vt.reshape(b, nb, MOBA_KV_HEADS * V_ROWS, MOBA_BLOCK)
    out = pl.pallas_call(
        functools.partial(_moba_prompt_kernel, n_near=n_near),
        grid=(b, nq),
        in_specs=[pl.BlockSpec((1, Q_PER_KV, KV_DIM, Q_BLOCK), lambda i, a: (i, 0, 0, a)),
                  pl.BlockSpec((1, nb, MOBA_BLOCK, KV_DIM), lambda i, a: (i, 0, 0, 0)),
                  pl.BlockSpec((1, nb, MOBA_KV_HEADS * V_ROWS, MOBA_BLOCK), lambda i, a: (i, 0, 0, 0)),
                  pl.BlockSpec((1, nb, KV_DIM), lambda i, a: (i, 0, 0)),
                  _vmem_full(), _vmem_full()],
        out_specs=pl.BlockSpec((1, MOBA_KV_HEADS, HEAD_DIM, hw), lambda i, a: (i, 0, 0, a)),
        out_shape=jax.ShapeDtypeStruct((b, MOBA_KV_HEADS, HEAD_DIM, nq * hw), BF16),
        scratch_shapes=[pltpu.VMEM((KV_DIM, w), BF16),
                        pltpu.VMEM((nb, w), F32),
                        pltpu.VMEM((MOBA_KV_HEADS, 1, hw), F32),
                        pltpu.VMEM((MOBA_KV_HEADS, V_ROWS, hw), F32),
                        pltpu.VMEM((MOBA_KV_HEADS, MOBA_BLOCK, hw), F32)],
        compiler_params=_cparams("parallel", "arbitrary"),
        name="moba_prompt",
    )(qt, kb, vt, km, bias_tiles, farb)
    out = out.reshape(b, MOBA_KV_HEADS, HEAD_DIM, nq, Q_PER_KV, Q_BLOCK).transpose(0, 3, 5, 1, 4, 2)
    return out.reshape(b, t, D_MODEL)


def _moba_sample_kernel(pt_ref, qx_ref, *rest, pages_per_step, n_far, dec_seq):
    del pt_ref
    gp = pages_per_step
    k_refs, v_refs = rest[:gp], rest[gp:2 * gp]
    (knew_ref, vnew_ref, biass_ref, farb_ref, o_ref,
     st_ref, vt_ref, gs_ref, add_ref) = rest[2 * gp:]
    tn = (((0,), (0,)), ((), ()))
    j = pl.program_id(1)
    nbp = st_ref.shape[0]
    n_near = nbp - n_far
    page = MOBA_BLOCK // 2
    qx = qx_ref[0]

    for i in range(gp):
        blk = j * (gp // 2) + i // 2
        half = i % 2
        keys = slice(half * page, (half + 1) * page)
        s = lax.dot_general(k_refs[i][0].astype(BF16), qx, tn, preferred_element_type=F32)
        st_ref[blk, keys, :] = s
        vt_ref[blk, :, keys] = v_refs[i][0].astype(BF16)
        ssum = jnp.sum(s, axis=0, keepdims=True)
        if half == 0:
            gs_ref[pl.ds(blk, 1), :] = ssum
        else:
            gs_ref[pl.ds(blk, 1), :] = gs_ref[pl.ds(blk, 1), :] + ssum

    @pl.when(j == pl.num_programs(1) - 1)
    def _fin():
        gate = gs_ref[...] * (1.0 / MOBA_BLOCK)
        sel, blk = _select_blocks(gate, nbp, nbp)
        add_ref[...] = jnp.where(sel > 0.0, jnp.where(blk < n_far, farb_ref[...], 0.0), NEG_INF)

        s_own = _dot(knew_ref[0].astype(BF16), qx) + biass_ref[n_near, 0:Q_BLOCK, :]
        m = jnp.max(s_own, axis=0, keepdims=True)

        def far_logits(b, m):
            lg = st_ref[b] + add_ref[pl.ds(b, 1), :]
            st_ref[b] = lg
            return jnp.maximum(m, jnp.max(lg, axis=0, keepdims=True))

        m = lax.fori_loop(0, n_far, far_logits, m)
        for r in range(n_near):
            b = n_far + r
            lg = st_ref[b] + add_ref[b:b + 1, :] + biass_ref[r]
            st_ref[b] = lg
            m = jnp.maximum(m, jnp.max(lg, axis=0, keepdims=True))

        p_own = jnp.exp2(s_own - m)
        l0 = jnp.sum(p_own, axis=0, keepdims=True)
        acc0 = _dot(vnew_ref[0].T.astype(BF16), p_own.astype(BF16))

        unroll = math.gcd(nbp, 4)

        def weighted(i, carry):
            l, acc = carry
            for u in range(unroll):
                b = i * unroll + u
                p = jnp.exp2(st_ref[b] - m)
                l = l + jnp.sum(p, axis=0, keepdims=True)
                acc = acc + _dot(vt_ref[b], p.astype(BF16))
            return l, acc

        l, acc = lax.fori_loop(0, nbp // unroll, weighted, (l0, acc0))
        y = acc / l
        lane_h = lax.broadcasted_iota(I32, (HEAD_DIM, LANES), 1) // (Q_PER_KV * dec_seq)
        out = jnp.zeros((HEAD_DIM, LANES), F32)
        for h in range(MOBA_KV_HEADS):
            out = jnp.where(lane_h == h, y[h * HEAD_DIM:(h + 1) * HEAD_DIM, :], out)
        o_ref[0] = out


def _moba_sample(q, k_new, v_new, cache_k, cache_v, page_table, bias_tiles, farb, n_far,
                 pages_per_step):
    s, dec_seq, _ = q.shape
    n_pool, page = cache_k.shape[:2]
    n_pages = page_table.shape[1]
    assert page * 2 == MOBA_BLOCK and n_pages % pages_per_step == 0 and pages_per_step % 2 == 0
    assert MOBA_HEADS * dec_seq == LANES and dec_seq <= Q_BLOCK
    nbp = n_pages // 2
    gp = pages_per_step
    q5 = q.reshape(s, dec_seq, MOBA_KV_HEADS, Q_PER_KV, HEAD_DIM)
    qx = jnp.einsum("sthgd,ph->spdhgt", q5, jnp.eye(MOBA_KV_HEADS, dtype=q.dtype)).reshape(
        s, KV_DIM, LANES)
    pad = ((0, 0), (0, Q_BLOCK - dec_seq), (0, 0))
    k_pad, v_pad = jnp.pad(k_new, pad), jnp.pad(v_new, pad)
    kt = cache_k.transpose(0, 2, 3, 1).reshape(n_pool, KV_DIM, page)
    vt = cache_v.transpose(0, 2, 3, 1).reshape(n_pool, KV_DIM, page)

    def page_spec(i):
        return pl.BlockSpec((1, KV_DIM, page), lambda b, j, pt: (pt[b, j * gp + i], 0, 0))

    seq_spec = lambda r, w: pl.BlockSpec((1, r, w), lambda b, j, pt: (b, 0, 0))
    grid_spec = pltpu.PrefetchScalarGridSpec(
        num_scalar_prefetch=1,
        grid=(s, n_pages // gp),
        in_specs=([seq_spec(KV_DIM, LANES)] + [page_spec(i) for i in range(gp)] * 2
                  + [seq_spec(Q_BLOCK, KV_DIM), seq_spec(Q_BLOCK, KV_DIM), _vmem_full(), _vmem_full()]),
        out_specs=seq_spec(HEAD_DIM, LANES),
        scratch_shapes=[pltpu.VMEM((nbp, MOBA_BLOCK, LANES), F32),
                        pltpu.VMEM((nbp, KV_DIM, MOBA_BLOCK), BF16),
                        pltpu.VMEM((nbp, LANES), F32),
                        pltpu.VMEM((nbp, LANES), F32)],
    )
    out = pl.pallas_call(
        functools.partial(_moba_sample_kernel, pages_per_step=gp, n_far=n_far, dec_seq=dec_seq),
        grid_spec=grid_spec,
        out_shape=jax.ShapeDtypeStruct((s, HEAD_DIM, LANES), F32),
        compiler_params=_cparams("parallel", "arbitrary"),
        name="moba_sample",
    )(page_table, qx, *([kt] * gp), *([vt] * gp), k_pad, v_pad, bias_tiles, farb)
    out = out.reshape(s, HEAD_DIM, MOBA_KV_HEADS, Q_PER_KV, dec_seq).transpose(0, 4, 2, 3, 1)
    return out.reshape(s, dec_seq, D_MODEL).astype(BF16)


def _memkv_kernel(mem_ref, gn_ref, w_ref, gk_ref, e_ref, et_ref, k_ref, v_ref):
    x = mem_ref[...]
    h = (x * lax.rsqrt(jnp.mean(x * x, axis=-1, keepdims=True) + EPS) * gn_ref[...]).astype(BF16)
    kv = _dot(h, w_ref[...])
    k = kv[:, :D_MODEL]
    k_ref[...] = k * _group_rr(k, e_ref, et_ref, MEM_HEAD_DIM) * gk_ref[...]
    v_ref[...] = kv[:, D_MODEL:]


def _memkv(mem, wts):
    n = mem.shape[0]
    tm = min(n, 256)
    assert n % tm == 0
    consts = wts["memkv"]
    row = pl.BlockSpec((tm, D_MODEL), lambda i: (i, 0))
    return pl.pallas_call(
        _memkv_kernel,
        grid=(n // tm,),
        in_specs=[row] + [_vmem_full()] * len(consts),
        out_specs=(row, row),
        out_shape=(jax.ShapeDtypeStruct((n, D_MODEL), F32),) * 2,
        compiler_params=_cparams("parallel"),
        name="mem_kv",
    )(mem, *consts)


def _memattn_kernel(q_ref, k_ref, v_ref, o_ref):
    q = q_ref[0]
    for h in range(MEM_HEADS):
        sl = slice(h * MEM_HEAD_DIM, (h + 1) * MEM_HEAD_DIM)
        kh = k_ref[0, :, sl].astype(BF16)
        vh = v_ref[0, :, sl].astype(BF16)
        s = lax.dot_general(q[:, sl], kh, (((1,), (1,)), ((), ())), preferred_element_type=F32)
        p = jnp.exp(s - jnp.max(s, axis=-1, keepdims=True))
        l = jnp.sum(p, axis=-1, keepdims=True)
        o_ref[0, :, sl] = (_dot(p.astype(BF16), vh) / l).astype(BF16)


def _memattn(qm, mk, mv, tq):
    b, t, _ = qm.shape
    m = mk.shape[1]
    assert t % tq == 0
    qspec = pl.BlockSpec((1, tq, D_MODEL), lambda i, a: (i, a, 0))
    mspec = pl.BlockSpec((1, m, D_MODEL), lambda i, a: (i, 0, 0))
    return pl.pallas_call(
        _memattn_kernel,
        grid=(b, t // tq),
        in_specs=[qspec, mspec, mspec],
        out_specs=qspec,
        out_shape=jax.ShapeDtypeStruct((b, t, D_MODEL), BF16),
        compiler_params=_cparams("parallel", "arbitrary"),
        name="mem_attn",
    )(qm, mk, mv)


def _merge_kernel(x_ref, ys_ref, ym_ref, yc_ref, gate_ref, ws_ref, wm_ref, wc_ref, wo_ref, gf_ref,
                  x1_ref, hn_ref):
    g = gate_ref[...].astype(F32)
    merged = (g[:, :D_MODEL] * _dot(ys_ref[...], ws_ref[...])
              + g[:, D_MODEL:2 * D_MODEL] * _dot(ym_ref[...], wm_ref[...])
              + g[:, 2 * D_MODEL:] * _dot(yc_ref[...], wc_ref[...]))
    x1 = x_ref[...] + _dot(merged.astype(BF16), wo_ref[...])
    x1_ref[...] = x1
    hn_ref[...] = x1 * lax.rsqrt(jnp.mean(x1 * x1, axis=-1, keepdims=True) + EPS) * gf_ref[...]


def _merge(x, y_ssd, y_moba, y_mem, gate, wts, tm):
    n = x.shape[0]
    assert n % tm == 0
    consts = wts["merge"]
    row = lambda w: pl.BlockSpec((tm, w), lambda i: (i, 0))
    return pl.pallas_call(
        _merge_kernel,
        grid=(n // tm,),
        in_specs=[row(D_MODEL), row(D_INNER), row(D_MODEL), row(D_MODEL), row(N_BRANCHES * D_MODEL)]
                 + [_vmem_full()] * len(consts),
        out_specs=(row(D_MODEL), row(D_MODEL)),
        out_shape=(jax.ShapeDtypeStruct((n, D_MODEL), F32),) * 2,
        compiler_params=_cparams("parallel"),
        name="merge_out",
    )(x, y_ssd, y_moba, y_mem, gate, *consts)


def _topk_rows(vals, payload, k, v_ref, p_ref):
    r = vals.shape[0]
    rows = lax.broadcasted_iota(I32, vals.shape, 0).astype(F32)
    for i in range(k):
        m = jnp.max(vals, axis=0, keepdims=True)
        pos = jnp.min(jnp.where(vals == m, rows, float(r)), axis=0, keepdims=True)
        hit = rows == pos
        if payload is None:
            p_ref[i:i + 1, :] = pos
        else:
            p_ref[i:i + 1, :] = jnp.max(jnp.where(hit, payload, -1.0), axis=0, keepdims=True)
        v_ref[i:i + 1, :] = m
        vals = jnp.where(hit, NEG_INF, vals)


CAND_COUNTS = tuple(PEER_TOPK // (a + 1) for a in range(PEER_TOPK))
CAND_ROWS = -(-sum(CAND_COUNTS) // SUBLANES) * SUBLANES


def _peer_topk_kernel(hn_ref, wpq_ref, keys_ref, eid_ref, g_ref,
                      s1v_ref, s1i_ref, s2v_ref, s2i_ref, cand_ref, cid_ref, tv_ref, ti_ref):
    qv = _dot(hn_ref[...].astype(BF16), wpq_ref[...])
    K = PEER_TOPK
    n_cand = sum(CAND_COUNTS)
    tm = cand_ref.shape[1]
    cand_ref[n_cand:, :] = jnp.full((CAND_ROWS - n_cand, tm), NEG_INF, F32)
    cid_ref[n_cand:, :] = jnp.zeros((CAND_ROWS - n_cand, tm), F32)
    for h in range(PEER_HEADS):
        for x, (sv_ref, si_ref) in enumerate(((s1v_ref, s1i_ref), (s2v_ref, s2i_ref))):
            c0 = (h * 2 + x) * PEER_HALF
            q_hi, q_lo = _split_bf16(qv[:, c0:c0 + PEER_HALF])
            ql = jnp.concatenate([q_hi, q_lo], axis=1)
            st = lax.dot_general(keys_ref[h * 2 + x], ql, (((1,), (1,)), ((), ())),
                                 preferred_element_type=F32)
            _topk_rows(st, None, K, sv_ref, si_ref)
        s1, i1, s2, i2 = s1v_ref[...], s1i_ref[...], s2v_ref[...], s2i_ref[...]
        r0 = 0
        for a, cnt in enumerate(CAND_COUNTS):
            cand_ref[r0:r0 + cnt, :] = s1[a:a + 1, :] + s2[0:cnt, :]
            cid_ref[r0:r0 + cnt, :] = i1[a:a + 1, :] * float(PEER_KEYS) + i2[0:cnt, :]
            r0 += cnt
        _topk_rows(cand_ref[...], cid_ref[...], K, tv_ref, ti_ref)
        top = tv_ref[...]
        e = jnp.exp(top - top[0:1, :])
        g_ref[h * K:(h + 1) * K, :] = e / jnp.sum(e, axis=0, keepdims=True)
        eid_ref[h * K:(h + 1) * K, :] = ti_ref[...].astype(I32)


def _peer_topk(hn, wts, tm):
    n = hn.shape[0]
    assert n % tm == 0
    consts = wts["peer_topk"]
    rows = PEER_HEADS * PEER_TOPK
    col = pl.BlockSpec((rows, tm), lambda i: (0, i))
    K = PEER_TOPK
    return pl.pallas_call(
        _peer_topk_kernel,
        grid=(n // tm,),
        in_specs=[pl.BlockSpec((tm, D_MODEL), lambda i: (i, 0))] + [_vmem_full()] * len(consts),
        out_specs=(col, col),
        out_shape=(jax.ShapeDtypeStruct((rows, n), I32), jax.ShapeDtypeStruct((rows, n), F32)),
        scratch_shapes=[pltpu.VMEM((K, tm), F32)] * 4
                       + [pltpu.VMEM((CAND_ROWS, tm), F32)] * 2
                       + [pltpu.VMEM((K, tm), F32)] * 2,
        compiler_params=_cparams("parallel"),
        name="peer_topk",
    )(hn, *consts)


TABLE_ROWS = 4
HI_MASK = -65536


def _pack_table(tbl):
    e = tbl.shape[0]
    bits = lax.bitcast_convert_type(tbl.astype(BF16), jnp.uint16).astype(jnp.uint32)
    half = D_MODEL // 2
    word = bits[:, :half] | (bits[:, half:] << 16)
    return lax.bitcast_convert_type(word, I32).reshape(e, TABLE_ROWS, LANES)


def _unpack_row(x):
    lo = pltpu.bitcast(x << 16, F32)
    hi = pltpu.bitcast(x & HI_MASK, F32)
    return lo, hi


def _peer_u_kernel(eid_ref, h_ref, g_ref, u_ref, fold_ref, coef_ref, p_ref, act4_ref):
    tt = h_ref.shape[0]
    n_sel = eid_ref.shape[1]
    grp = p_ref.shape[1] // LANES
    row_tok = lax.broadcasted_iota(I32, (grp * LANES, tt), 0) // LANES
    col_i = lax.broadcasted_iota(I32, (grp * LANES, tt), 1)
    act4_ref[...] = jnp.zeros(act4_ref.shape, F32)

    def token_group(gi, carry):
        t0 = gi * grp
        for g in range(grp):
            hh = h_ref[t0 + g]
            h_lo, h_hi = hh[0:TABLE_ROWS], hh[TABLE_ROWS:]
            for j in range(n_sel):
                lo, hi = _unpack_row(u_ref[eid_ref[t0 + g, j]])
                p_ref[j * TABLE_ROWS:(j + 1) * TABLE_ROWS, g * LANES:(g + 1) * LANES] = (
                    lo * h_lo + hi * h_hi)
        onehot = jnp.where(col_i == t0 + row_tok, 1.0, 0.0).astype(BF16)
        act4_ref[...] += _dot(p_ref[...].astype(BF16), onehot)
        return carry

    lax.fori_loop(0, tt // grp, token_group, 0)
    a_hi, a_lo = _split_bf16(act4_ref[...])
    act = _dot(fold_ref[...], a_hi) + _dot(fold_ref[...], a_lo)
    gelu = 0.5 * act * (1.0 + lax.erf(act * (1.0 / math.sqrt(2.0))))
    coef_ref[...] = g_ref[...] * gelu


N_ACC = 2


def _peer_v_kernel(eid_ref, coef_ref, x1_ref, v_ref, o_ref, crep_ref):
    tt = x1_ref.shape[0]
    n_sel = eid_ref.shape[1]
    lane = lax.broadcasted_iota(I32, (n_sel, tt), 1)

    def spread(t, buf):
        c = jnp.sum(jnp.where(lane == t, coef_ref[...], 0.0), axis=-1, keepdims=True)
        crep_ref[buf] = jnp.broadcast_to(c, (n_sel, LANES))

    def weighted_rows(t, buf):
        acc_lo = [jnp.zeros((TABLE_ROWS, LANES), F32) for _ in range(N_ACC)]
        acc_hi = [jnp.zeros((TABLE_ROWS, LANES), F32) for _ in range(N_ACC)]
        for j in range(n_sel):
            lo, hi = _unpack_row(v_ref[eid_ref[t, j]])
            cj = crep_ref[buf, j:j + 1, :]
            acc_lo[j % N_ACC] = acc_lo[j % N_ACC] + cj * lo
            acc_hi[j % N_ACC] = acc_hi[j % N_ACC] + cj * hi
        ff = jnp.concatenate([sum(acc_lo[1:], acc_lo[0]), sum(acc_hi[1:], acc_hi[0])], axis=0)
        o_ref[t] = x1_ref[t] + ff

    n_buf = crep_ref.shape[0]

    def token_group(i, carry):
        t0 = n_buf * i
        for u in range(1, n_buf):
            spread(t0 + u, u)
        weighted_rows(t0, 0)
        spread(jnp.minimum(t0 + n_buf, tt - 1), 0)
        for u in range(1, n_buf):
            weighted_rows(t0 + u, u)
        return carry

    spread(0, 0)
    lax.fori_loop(0, tt // n_buf, token_group, 0)


def _peer(x1, hn, wts, tm_topk, tt):
    n = x1.shape[0]
    grp = 16 if tt % 16 == 0 else 1
    assert n % tt == 0 and tt % 2 == 0
    n_sel = PEER_HEADS * PEER_TOPK
    eid_t, g_t = _peer_topk(hn, wts, tm_topk)
    eid = eid_t.T
    fold = jnp.asarray(np.repeat(np.eye(n_sel, dtype=np.float32), TABLE_ROWS, axis=1), BF16)
    tok3 = pl.BlockSpec((tt, SUBLANES, LANES), lambda i: (i, 0, 0))
    col = pl.BlockSpec((n_sel, tt), lambda i: (0, i))
    ids = pl.BlockSpec((tt, n_sel), lambda i: (i, 0), memory_space=pltpu.SMEM)
    coef = pl.pallas_call(
        _peer_u_kernel,
        grid=(n // tt,),
        in_specs=[ids, tok3, col, _vmem_full(), _vmem_full()],
        out_specs=col,
        out_shape=jax.ShapeDtypeStruct((n_sel, n), F32),
        scratch_shapes=[pltpu.VMEM((n_sel * TABLE_ROWS, grp * LANES), F32),
                        pltpu.VMEM((n_sel * TABLE_ROWS, tt), F32)],
        compiler_params=_cparams("parallel"),
        name="peer_u",
    )(eid, hn.reshape(n, SUBLANES, LANES), g_t, wts["peer_u"], fold)
    out = pl.pallas_call(
        _peer_v_kernel,
        grid=(n // tt,),
        in_specs=[ids, col, tok3, _vmem_full()],
        out_specs=tok3,
        out_shape=jax.ShapeDtypeStruct((n, SUBLANES, LANES), F32),
        scratch_shapes=[pltpu.VMEM((2, n_sel, LANES), F32)],
        compiler_params=_cparams("parallel"),
        name="peer_v",
    )(eid, coef, x1.reshape(n, SUBLANES, LANES), wts["peer_v"])
    return out.reshape(n, D_MODEL)


IN_WIDTHS = (D_INNER, CONV_DIM, SSM_HEADS, D_MODEL, KV_DIM, KV_DIM, D_MODEL, N_BRANCHES * D_MODEL)


def _row(v):
    return v.reshape(1, -1).astype(F32)


def _prepare(p):
    offs = np.cumsum((0,) + IN_WIDTHS)
    seg = [p["w_in"][:, offs[i]:offs[i + 1]].astype(BF16) for i in range(len(IN_WIDTHS))]
    wz, wxbc, wdt, wq, wk, wv, wqm, wg = seg
    wdt = jnp.pad(wdt, ((0, 0), (0, LANES - SSM_HEADS)))
    e64, et64 = _group_indicators(D_MODEL, HEAD_DIM)
    e64k, et64k = _group_indicators(KV_DIM, HEAD_DIM)
    e256, et256 = _group_indicators(D_MODEL, MEM_HEAD_DIM)
    gq = _row(jnp.tile(p["g_q"], MOBA_HEADS)) * (HEAD_DIM ** -0.5 * LOG2E)
    gk = _row(jnp.tile(p["g_k"], MOBA_KV_HEADS))
    gmq = _row(jnp.tile(p["g_mq"], MEM_HEADS)) * MEM_HEAD_DIM ** -0.5
    inproj = [_row(p["g_mix_norm"]), wz, wxbc, wdt, wq, wk, wv, wqm, wg, _row(p["b_gate"]),
              gq, gk, gmq, e64, et64, e64k, et64k, e256, et256]

    pad_h = lambda v: jnp.pad(_row(v), ((0, 0), (0, LANES - SSM_HEADS)))
    e2 = np.zeros((LANES, D_INNER), np.float32)
    e2[np.arange(D_INNER) // SSM_HEAD_DIM, np.arange(D_INNER)] = 1.0
    e2 = jnp.asarray(np.concatenate([e2, e2], axis=0), BF16)
    eg, etg2 = _group_indicators(D_INNER, D_INNER // SSM_GROUPS)
    ssd = [p["conv_w"].astype(F32), _row(p["conv_b"]), pad_h(p["dt_bias"]), pad_h(p["a_log"]),
           _row(jnp.repeat(p["d_skip"], SSM_HEAD_DIM)), _row(p["g_ssm_norm"]), e2, eg, etg2]

    memkv = [_row(p["g_mem_norm"]), p["w_mem_kv"].astype(BF16),
             _row(jnp.tile(p["g_mk"], MEM_HEADS)), e256, et256]
    merge = [p["w_ssm_out"].astype(BF16), p["w_moba_out"].astype(BF16), p["w_mem_out"].astype(BF16),
             p["w_out"].astype(BF16), _row(p["g_ffn_norm"])]
    keys = p["peer_keys"].astype(BF16).reshape(PEER_HEADS * 2, PEER_KEYS, PEER_HALF)
    peer_topk = [p["w_pq"].astype(BF16), jnp.concatenate([keys, keys], axis=-1)]
    return {"inproj": inproj, "ssd": ssd, "memkv": memkv, "merge": merge, "peer_topk": peer_topk,
            "peer_u": _pack_table(p["peer_u"]), "peer_v": _pack_table(p["peer_v"])}


def _tile(n, pref):
    return pref if n % pref == 0 else n


def _group_forward(x, wts, ssd_state, attn_fn, mem_k, mem_v):
    b, t, _ = x.shape
    n = b * t
    x2 = x.reshape(n, D_MODEL)
    z, xbc, dt, q, k, v, qm, gate = _inproj(x2, wts, _tile(n, 256))
    conv0, ssm0, q_rows, n_valid = ssd_state
    y_ssd, ssm_new, conv_new = _ssd(xbc.reshape(b, t, CONV_DIM), z.reshape(b, t, D_INNER),
                                    dt.reshape(b, t, LANES), conv0, ssm0, wts, q_rows, n_valid)
    k3, v3 = k.reshape(b, t, KV_DIM), v.reshape(b, t, KV_DIM)
    y_moba = attn_fn(q.reshape(b, t, D_MODEL), k3, v3)
    y_mem = _memattn(qm.reshape(b, t, D_MODEL), mem_k, mem_v, _tile(t, 512))
    x1, hn = _merge(x2, y_ssd.reshape(n, D_INNER), y_moba.reshape(n, D_MODEL),
                    y_mem.reshape(n, D_MODEL), gate, wts, _tile(n, 256))
    out = _peer(x1, hn, wts, _tile(n, 256), _tile(n, 128))
    return (out.reshape(b, t, D_MODEL),
            ssm_new.reshape(b, SSM_HEADS, SSM_HEAD_DIM, D_STATE), conv_new,
            k3.reshape(b, t, MOBA_KV_HEADS, HEAD_DIM), v3.reshape(b, t, MOBA_KV_HEADS, HEAD_DIM))


def kernel(x_prompt, x_sample, cache_k, cache_v, state_ssm, state_conv, cache_mem_k, cache_mem_v,
           page_table, mem_prompt, g_mix_norm, w_in, conv_w, conv_b, dt_bias, a_log, d_skip,
           g_ssm_norm, w_ssm_out, g_q, g_k, rel_bias, w_moba_out, g_mem_norm, w_mem_kv, g_mq, g_mk,
           w_mem_out, b_gate, w_out, g_ffn_norm, w_pq, peer_keys, peer_u, peer_v):
    depth = w_in.shape[0]
    bp, tp, _ = x_prompt.shape
    sb, dec_seq, _ = x_sample.shape
    mem_len = mem_prompt.shape[1]
    n_pool, page = cache_k.shape[1], cache_k.shape[2]
    past_len = page_table.shape[1] * page
    assert past_len % MOBA_BLOCK == 0 and tp % MOBA_BLOCK == 0

    far_d = _far_distance()
    n_near_p = -(-(far_d + MOBA_BLOCK - 1) // MOBA_BLOCK) - 1
    bias_p = _bias_prompt(rel_bias, 2 * n_near_p + 2)
    farb_p = _row(jnp.repeat(rel_bias[NUM_BUCKETS - 1], Q_BLOCK)) * LOG2E
    nbp = past_len // MOBA_BLOCK
    n_far_s = min(max((past_len - (MOBA_BLOCK - 1) - far_d) // MOBA_BLOCK + 1, 0), nbp)
    bias_s = _bias_sample(rel_bias, past_len, n_far_s, nbp - n_far_s, dec_seq)
    farb_s = _row(jnp.repeat(rel_bias[NUM_BUCKETS - 1], dec_seq)) * LOG2E

    per_layer = dict(g_mix_norm=g_mix_norm, w_in=w_in, conv_w=conv_w, conv_b=conv_b, dt_bias=dt_bias,
                     a_log=a_log, d_skip=d_skip, g_ssm_norm=g_ssm_norm, w_ssm_out=w_ssm_out, g_q=g_q,
                     g_k=g_k, w_moba_out=w_moba_out, g_mem_norm=g_mem_norm, w_mem_kv=w_mem_kv,
                     g_mq=g_mq, g_mk=g_mk, w_mem_out=w_mem_out, b_gate=b_gate, w_out=w_out,
                     g_ffn_norm=g_ffn_norm, w_pq=w_pq, peer_keys=peer_keys, peer_u=peer_u,
                     peer_v=peer_v)
    xp, xs = x_prompt, x_sample
    outs = [[] for _ in range(10)]
    chunk = math.gcd(tp, SSD_CHUNK)
    for l in range(depth):
        wts = _prepare({name: val[l] for name, val in per_layer.items()})
        mk, mv = _memkv(mem_prompt.reshape(bp * mem_len, D_MODEL), wts)
        mk, mv = mk.reshape(bp, mem_len, D_MODEL), mv.reshape(bp, mem_len, D_MODEL)
        prompt_attn = functools.partial(_moba_prompt, bias_tiles=bias_p, farb=farb_p, n_near=n_near_p)
        xp, h_p, c_p, k_p, v_p = _group_forward(xp, wts, (None, None, chunk, chunk), prompt_attn, mk, mv)
        sample_attn = functools.partial(
            _moba_sample, cache_k=cache_k[l], cache_v=cache_v[l], page_table=page_table,
            bias_tiles=bias_s, farb=farb_s, n_far=n_far_s, pages_per_step=8)
        conv0 = jnp.pad(state_conv[l], ((0, 0), (SUBLANES - (CONV_WIDTH - 1), 0), (0, 0)))
        ssm0 = state_ssm[l].reshape(sb, D_INNER, D_STATE)
        xs, h_s, c_s, k_s, v_s = _group_forward(
            xs, wts, (conv0, ssm0, Q_BLOCK, dec_seq), sample_attn,
            cache_mem_k[l].reshape(sb, mem_len, D_MODEL), cache_mem_v[l].reshape(sb, mem_len, D_MODEL))
        heads = (MEM_HEADS, MEM_HEAD_DIM)
        for lst, val in zip(outs, (k_p, v_p, k_s, v_s, h_p, h_s, c_p, c_s,
                                   mk.reshape(bp, mem_len, *heads), mv.reshape(bp, mem_len, *heads))):
            lst.append(val)
    return (xp, xs) + tuple(jnp.stack(lst) for lst in outs)
```
